```python
import jax, jax.numpy as jnp
from jax import lax
import numpy as np

D_MODEL = 1024
BATCH = 2
SEQ = 8192
DEPTH = 1
DEC_BATCH = 32
DEC_SEQ = 32
PAST_LEN = 2048

CHUNK = 64
D_MIX = D_MODEL
RET_WIDTH = D_MIX // 2
RET_HEADS = 4
RET_DK = RET_WIDTH // RET_HEADS
RET_DV = RET_WIDTH // RET_HEADS
CONV_WIDTH = D_MIX - RET_WIDTH
CONV_K = 31
IN_COLS = 4 * RET_WIDTH + 2 * CONV_WIDTH
ROPE_BASE = 10000.0
N_EXPERTS = 32
TOP_K = 4
D_FF = D_MODEL
SWIGLU_LIMIT = 7.0
SWIGLU_ALPHA = 1.702
EPS = 1e-6

kernel_name = 'hybrid_retention_conformer_moe_stream_step'


def rms_norm(x, g):
    xf = x.astype(jnp.float32)
    y = xf * lax.rsqrt(jnp.mean(xf * xf, axis=-1, keepdims=True) + EPS)
    return (y * g.astype(jnp.float32)).astype(x.dtype)


def layer_norm(x, g, b):
    xf = x.astype(jnp.float32)
    mu = jnp.mean(xf, axis=-1, keepdims=True)
    xc = xf - mu
    y = xc * lax.rsqrt(jnp.mean(xc * xc, axis=-1, keepdims=True) + EPS)
    return (y * g.astype(jnp.float32) + b.astype(jnp.float32)).astype(x.dtype)


def rotary(x, pos):
    half = x.shape[-1] // 2
    inv = ROPE_BASE ** (-jnp.arange(half, dtype=jnp.float32) / half)
    ang = pos[:, None] * inv[None, :]
    cos = jnp.cos(ang)[None, :, None, :].astype(x.dtype)
    sin = jnp.sin(ang)[None, :, None, :].astype(x.dtype)
    x1, x2 = x[..., :half], x[..., half:]
    return jnp.concatenate([x1 * cos - x2 * sin, x2 * cos + x1 * sin], axis=-1)


def retention_log_decay():
    return jnp.log(1.0 - 2.0 ** (-5.0 - jnp.arange(RET_HEADS, dtype=jnp.float32)))


def chunk_retention(q, k, v, s0):
    b, l, h, _ = q.shape
    dv = v.shape[-1]
    lc = min(CHUNK, l)
    nc = l // lc
    dt = q.dtype

    def to_chunks(t):
        return t.reshape(b, nc, lc, h, t.shape[-1]).transpose(0, 3, 1, 2, 4)

    qc, kc, vc = to_chunks(q), to_chunks(k), to_chunks(v)
    lg = retention_log_decay()
    idx = jnp.arange(lc, dtype=jnp.float32)
    intra = jnp.exp(lg[:, None, None] * jnp.abs(idx[:, None] - idx[None, :])).astype(dt)
    q_dec = jnp.exp(lg[:, None] * (idx + 1.0)).astype(dt)
    k_dec = jnp.exp(lg[:, None] * (lc - 1.0 - idx)).astype(dt)
    chunk_dec = jnp.exp(lg * lc).astype(dt)

    scores = jnp.einsum('bhcid,bhcjd->bhcij', qc, kc) * intra[:, None]
    o_intra = jnp.einsum('bhcij,bhcjv->bhciv', scores, vc)
    u = jnp.einsum('bhcjd,bhcjv->bhcdv', kc * k_dec[:, None, :, None], vc)

    def step(s, u_c):
        return chunk_dec[None, :, None, None] * s + u_c, s

    s_last, s_prev = lax.scan(step, s0.astype(dt), jnp.moveaxis(u, 2, 0))
    s_prev = jnp.moveaxis(s_prev, 0, 2)
    o_cross = jnp.einsum('bhcid,bhcdv->bhciv', qc * q_dec[:, None, :, None], s_prev)
    o = (o_intra + o_cross).transpose(0, 2, 3, 1, 4).reshape(b, l, h, dv)
    return o, s_last


def causal_depthwise_conv(u, buf, w, bias):
    full = jnp.concatenate([buf.astype(u.dtype), u], axis=1)
    out = lax.conv_general_dilated(full, w[:, None, :].astype(u.dtype), window_strides=(1,),
                                   padding='VALID', dimension_numbers=('NWC', 'WIO', 'NWC'),
                                   feature_group_count=u.shape[-1])
    return out + bias, full[:, -(CONV_K - 1):]


def token_mixer(xn, pos, s0, buf, w_in, conv_w, conv_b, conv_ln_g, conv_ln_b, ret_norm_g, w_out):
    b, l, _ = xn.shape
    proj = xn @ w_in
    r, c = RET_WIDTH, CONV_WIDTH
    q, k, v, g, a, ga = jnp.split(proj, [r, 2 * r, 3 * r, 4 * r, 4 * r + c], axis=-1)

    q = rotary(q.reshape(b, l, RET_HEADS, RET_DK), pos)
    k = rotary(k.reshape(b, l, RET_HEADS, RET_DK), pos) * (RET_DK ** -0.5)
    o, s_new = chunk_retention(q, k, v.reshape(b, l, RET_HEADS, RET_DV), s0)
    of = o.astype(jnp.float32)
    of = of * lax.rsqrt(jnp.mean(of * of, axis=-1, keepdims=True) + EPS)
    o_ret = (of.reshape(b, l, r) * ret_norm_g.astype(jnp.float32)).astype(xn.dtype) * jax.nn.silu(g)

    u = a * jax.nn.sigmoid(ga)
    cv, buf_new = causal_depthwise_conv(u, buf, conv_w, conv_b)
    o_conv = jax.nn.silu(layer_norm(cv, conv_ln_g, conv_ln_b))

    return jnp.concatenate([o_ret, o_conv], axis=-1) @ w_out, s_new, buf_new


def moe_ffn(xn, w_router, b_router, w_e_gate, b_e_gate, w_e_up, b_e_up, w_e_down, b_e_down):
    shp = xn.shape
    t = xn.reshape(-1, shp[-1])
    logits = (t @ w_router).astype(jnp.float32) + b_router.astype(jnp.float32)
    top_v, top_i = lax.top_k(logits, TOP_K)
    wts = jax.nn.softmax(top_v, axis=-1)
    gates = jnp.einsum('nk,nke->ne', wts, jax.nn.one_hot(top_i, N_EXPERTS, dtype=jnp.float32)).astype(t.dtype)

    def expert(acc, p):
        wg, bg, wu, bu, wd, bd, gc = p
        hg = jnp.minimum(t @ wg + bg, SWIGLU_LIMIT)
        hu = jnp.clip(t @ wu + bu, -SWIGLU_LIMIT, SWIGLU_LIMIT)
        act = hg * jax.nn.sigmoid(SWIGLU_ALPHA * hg) * (hu + 1.0)
        return acc + gc[:, None] * (act @ wd + bd), None

    out, _ = lax.scan(expert, jnp.zeros_like(t),
                      (w_e_gate, b_e_gate, w_e_up, b_e_up, w_e_down, b_e_down, gates.T))
    return out.reshape(shp)


def run_trunk(x, pos, ret_state, conv_state, norm_mix_g, w_in, conv_w, conv_b, conv_ln_g, conv_ln_b,
              ret_norm_g, w_out, norm_ffn_g, w_router, b_router, w_e_gate, b_e_gate, w_e_up, b_e_up,
              w_e_down, b_e_down, norm_final_g):
    ret_new, conv_new = [], []
    for i in range(DEPTH):
        mix, s_i, c_i = token_mixer(rms_norm(x, norm_mix_g[i]), pos, ret_state[i], conv_state[i],
                                    w_in[i], conv_w[i], conv_b[i], conv_ln_g[i], conv_ln_b[i],
                                    ret_norm_g[i], w_out[i])
        x = x + mix
        x = x + moe_ffn(rms_norm(x, norm_ffn_g[i]), w_router[i], b_router[i], w_e_gate[i], b_e_gate[i],
                        w_e_up[i], b_e_up[i], w_e_down[i], b_e_down[i])
        ret_new.append(s_i)
        conv_new.append(c_i)
    return rms_norm(x, norm_final_g), jnp.stack(ret_new), jnp.stack(conv_new)


def setup_inputs(seed: int = 0) -> dict:
    key = jax.random.key(seed)
    ks = jax.random.split(key, 24)

    def nrm(k, shape, scale):
        return jax.random.normal(k, shape, dtype=jnp.float32) * scale

    return {
        'x_prompt': nrm(ks[0], (BATCH, SEQ, D_MODEL), 1.0),
        'x_sample': nrm(ks[1], (DEC_BATCH, DEC_SEQ, D_MODEL), 1.0),
        'state_ret': nrm(ks[2], (DEPTH, DEC_BATCH, RET_HEADS, RET_DK, RET_DV), 0.3),
        'cache_conv': nrm(ks[3], (DEPTH, DEC_BATCH, CONV_K - 1, CONV_WIDTH), 0.5),
        'norm_mix_g': 1.0 + nrm(ks[4], (DEPTH, D_MODEL), 0.02),
        'w_in': nrm(ks[5], (DEPTH, D_MODEL, IN_COLS), D_MODEL ** -0.5),
        'conv_w': nrm(ks[6], (DEPTH, CONV_K, CONV_WIDTH), CONV_K ** -0.5),
        'conv_b': nrm(ks[7], (DEPTH, CONV_WIDTH), 0.02),
        'conv_ln_g': 1.0 + nrm(ks[8], (DEPTH, CONV_WIDTH), 0.02),
        'conv_ln_b': nrm(ks[9], (DEPTH, CONV_WIDTH), 0.02),
        'ret_norm_g': 1.0 + nrm(ks[10], (DEPTH, RET_WIDTH), 0.02),
        'w_out': nrm(ks[11], (DEPTH, D_MIX, D_MODEL), D_MIX ** -0.5),
        'norm_ffn_g': 1.0 + nrm(ks[12], (DEPTH, D_MODEL), 0.02),
        'w_router': nrm(ks[13], (DEPTH, D_MODEL, N_EXPERTS), D_MODEL ** -0.5),
        'b_router': nrm(ks[14], (DEPTH, N_EXPERTS), 0.01),
        'w_e_gate': nrm(ks[15], (DEPTH, N_EXPERTS, D_MODEL, D_FF), D_MODEL ** -0.5),
        'b_e_gate': nrm(ks[16], (DEPTH, N_EXPERTS, D_FF), 0.02),
        'w_e_up': nrm(ks[17], (DEPTH, N_EXPERTS, D_MODEL, D_FF), D_MODEL ** -0.5),
        'b_e_up': nrm(ks[18], (DEPTH, N_EXPERTS, D_FF), 0.02),
        'w_e_down': nrm(ks[19], (DEPTH, N_EXPERTS, D_FF, D_MODEL), D_FF ** -0.5),
        'b_e_down': nrm(ks[20], (DEPTH, N_EXPERTS, D_MODEL), 0.02),
        'norm_final_g': 1.0 + nrm(ks[21], (D_MODEL,), 0.02),
    }


def reference(x_prompt, x_sample, state_ret, cache_conv, norm_mix_g, w_in, conv_w, conv_b, conv_ln_g,
              conv_ln_b, ret_norm_g, w_out, norm_ffn_g, w_router, b_router, w_e_gate, b_e_gate, w_e_up,
              b_e_up, w_e_down, b_e_down, norm_final_g):
    bp, lp, _ = x_prompt.shape
    ls = x_sample.shape[1]
    ret0 = jnp.zeros((DEPTH, bp, RET_HEADS, RET_DK, RET_DV), x_prompt.dtype)
    conv0 = jnp.zeros((DEPTH, bp, CONV_K - 1, CONV_WIDTH), x_prompt.dtype)
    pos_p = jnp.arange(lp, dtype=jnp.float32)
    y_prompt, state_ret_p, cache_conv_p = run_trunk(
        x_prompt, pos_p, ret0, conv0, norm_mix_g, w_in, conv_w, conv_b, conv_ln_g, conv_ln_b,
        ret_norm_g, w_out, norm_ffn_g, w_router, b_router, w_e_gate, b_e_gate, w_e_up, b_e_up,
        w_e_down, b_e_down, norm_final_g)
    pos_s = jnp.arange(ls, dtype=jnp.float32) + PAST_LEN
    y_sample, state_ret_s, cache_conv_s = run_trunk(
        x_sample, pos_s, state_ret, cache_conv, norm_mix_g, w_in, conv_w, conv_b, conv_ln_g, conv_ln_b,
        ret_norm_g, w_out, norm_ffn_g, w_router, b_router, w_e_gate, b_e_gate, w_e_up, b_e_up,
        w_e_down, b_e_down, norm_final_g)
    return (y_prompt, y_sample, state_ret_p, cache_conv_p, state_ret_s, cache_conv_s)
```

```python
import functools

import jax
import jax.numpy as jnp
from jax import lax
from jax.experimental import pallas as pl
from jax.experimental.pallas import tpu as pltpu

D_MODEL = 1024
CHUNK = 64
RET_WIDTH = 512
RET_HEADS = 4
RET_DK = 128
CONV_WIDTH = 512
CONV_K = 31
IN_COLS = 4 * RET_WIDTH + 2 * CONV_WIDTH
ROPE_BASE = 10000.0
N_EXPERTS = 32
TOP_K = 4
SWIGLU_LIMIT = 7.0
SWIGLU_ALPHA = 1.702
EPS = 1e-6
PAST_LEN = 2048

TILE = 256
CONV_PAD = 32
CONV_ROWS = 32
MOE_TILE = 256
VMEM_LIMIT = 52 * 1024 * 1024

F32 = jnp.float32
BF16 = jnp.bfloat16


def _dot(a, b):
    return jnp.dot(a, b, preferred_element_type=F32)


def _dot_nt(a, b):
    return lax.dot_general(a, b, (((1,), (1,)), ((), ())), preferred_element_type=F32)


def _dot_tn(a, b):
    return lax.dot_general(a, b, (((0,), (0,)), ((), ())), preferred_element_type=F32)


def _rms(x, g):
    return x * lax.rsqrt(jnp.mean(x * x, axis=-1, keepdims=True) + EPS) * g


def _sigmoid(x):
    return 1.0 / (1.0 + jnp.exp(-x))


def _mixer_body(nseg, seg, first, x_ref, cos_ref, sin_ref, mask_ref, qdec_ref, kdec_ref, cdec_ref,
                s_in, c_in, gmix_ref, win_ref, convw_ref, convb_ref, lng_ref, lnb_ref, retg_ref,
                wout_ref, gffn_ref, wr_ref, br_ref,
                h_ref, hn_ref, idx_ref, wts_ref, s_out, c_out, ubuf, s_scr):
    carry = first is not None
    T = nseg * seg
    x = x_ref[...]
    xn = _rms(x, gmix_ref[...]).astype(BF16)
    proj = _dot(xn, win_ref[...])
    cosf = cos_ref[...]
    sinf = sin_ref[...]

    if carry:
        @pl.when(first)
        def _():
            s_scr[...] = jnp.zeros_like(s_scr)
            ubuf[0:CONV_PAD, :] = jnp.zeros((CONV_PAD, CONV_WIDTH), F32)

    o_heads = []
    for h in range(RET_HEADS):
        q = proj[:, h * RET_DK:(h + 1) * RET_DK]
        k = proj[:, RET_WIDTH + h * RET_DK:RET_WIDTH + (h + 1) * RET_DK]
        v = proj[:, 2 * RET_WIDTH + h * RET_DK:2 * RET_WIDTH + (h + 1) * RET_DK].astype(BF16)
        q = q * cosf + pltpu.roll(q, RET_DK // 2, 1) * sinf
        k = (k * cosf + pltpu.roll(k, RET_DK // 2, 1) * sinf) * (RET_DK ** -0.5)
        p = (_dot_nt(q.astype(BF16), k.astype(BF16)) * mask_ref[h]).astype(BF16)
        o = _dot(p, v)
        qd = (q * qdec_ref[h]).astype(BF16)
        kd = (k * kdec_ref[h]).astype(BF16)
        cdec = cdec_ref[h, 0:1, :]
        o_cross = []
        for s in range(nseg):
            rows = slice(s * seg, (s + 1) * seg)
            s_prev = s_scr[h] if carry else s_in[0, s, h]
            o_cross.append(_dot(qd[rows], s_prev.astype(BF16)))
            s_new = cdec * s_prev + _dot_tn(kd[rows], v[rows])
            if carry:
                s_scr[h] = s_new
                s_out[0, 0, h] = s_new
            else:
                s_out[0, s, h] = s_new
        o = o + (o_cross[0] if nseg == 1 else jnp.concatenate(o_cross, axis=0))
        o = o * lax.rsqrt(jnp.mean(o * o, axis=-1, keepdims=True) + EPS)
        g = proj[:, 3 * RET_WIDTH + h * RET_DK:3 * RET_WIDTH + (h + 1) * RET_DK]
        o_heads.append(o * retg_ref[:, h * RET_DK:(h + 1) * RET_DK] * (g * _sigmoid(g)))

    a = proj[:, 4 * RET_WIDTH:4 * RET_WIDTH + CONV_WIDTH]
    ga = proj[:, 4 * RET_WIDTH + CONV_WIDTH:]
    u = a * _sigmoid(ga)
    hist = CONV_PAD - (CONV_K - 1)
    cv_chunks = []
    for s in range(nseg):
        if not carry:
            ubuf[hist:CONV_PAD, :] = c_in[0, s]
        ubuf[CONV_PAD:CONV_PAD + seg, :] = u[s * seg:(s + 1) * seg]
        for c in range(seg // CONV_ROWS):
            acc = jnp.broadcast_to(convb_ref[...], (CONV_ROWS, CONV_WIDTH))
            for j in range(CONV_K):
                r0 = hist + j + c * CONV_ROWS
                acc = acc + convw_ref[j:j + 1, :] * ubuf[r0:r0 + CONV_ROWS, :]
            cv_chunks.append(acc)
        tail = ubuf[hist + seg:CONV_PAD + seg, :]
        if carry:
            c_out[0, 0] = tail
            ubuf[0:CONV_PAD, :] = ubuf[seg:seg + CONV_PAD, :]
        else:
            c_out[0, s] = tail
    cv = jnp.concatenate(cv_chunks, axis=0)
    mu = jnp.mean(cv, axis=-1, keepdims=True)
    xc = cv - mu
    ln = xc * lax.rsqrt(jnp.mean(xc * xc, axis=-1, keepdims=True) + EPS) * lng_ref[...] + lnb_ref[...]
    o_conv = ln * _sigmoid(ln)

    mixed = jnp.concatenate(o_heads + [o_conv], axis=-1).astype(BF16)
    hres = x + _dot(mixed, wout_ref[...])
    h_ref[...] = hres

    hn = _rms(hres, gffn_ref[...])
    hn_ref[...] = hn
    logits = jnp.dot(hn, wr_ref[...], precision=lax.Precision.HIGHEST,
                     preferred_element_type=F32) + br_ref[...]
    lane = lax.broadcasted_iota(jnp.int32, (T, N_EXPERTS), 1).astype(F32)
    col = lax.broadcasted_iota(jnp.int32, (T, TOP_K), 1)
    idx_out = jnp.zeros((T, TOP_K), F32)
    val_out = jnp.zeros((T, TOP_K), F32)
    work = logits
    for kk in range(TOP_K):
        m = jnp.max(work, axis=-1, keepdims=True)
        sel = jnp.min(jnp.where(work == m, lane, float(N_EXPERTS)), axis=-1, keepdims=True)
        idx_out = jnp.where(col == kk, sel, idx_out)
        val_out = jnp.where(col == kk, m, val_out)
        work = jnp.where(lane == sel, -jnp.inf, work)
    ex = jnp.exp(val_out - val_out[:, 0:1])
    wts_ref[...] = ex / jnp.sum(ex, axis=-1, keepdims=True)
    idx_ref[...] = idx_out.astype(jnp.int32)


def _mixer_kernel(n_prompt_tiles, tiles_per_seq, nseg, seg,
                  xp_ref, xs_ref, cosp_ref, sinp_ref, coss_ref, sins_ref,
                  maskp_ref, qdecp_ref, kdecp_ref, cdecp_ref,
                  masks_ref, qdecs_ref, kdecs_ref, cdecs_ref, s_in, c_in,
                  gmix_ref, win_ref, convw_ref, convb_ref, lng_ref, lnb_ref, retg_ref,
                  wout_ref, gffn_ref, wr_ref, br_ref,
                  h_ref, hn_ref, idx_ref, wts_ref, sp_out, cp_out, ss_out, cs_out, ubuf, s_scr):
    i = pl.program_id(0)
    weights = (gmix_ref, win_ref, convw_ref, convb_ref, lng_ref, lnb_ref, retg_ref,
               wout_ref, gffn_ref, wr_ref, br_ref)
    outs = (h_ref, hn_ref, idx_ref, wts_ref)

    @pl.when(i < n_prompt_tiles)
    def _():
        _mixer_body(1, TILE, i % tiles_per_seq == 0, xp_ref, cosp_ref, sinp_ref,
                    maskp_ref, qdecp_ref, kdecp_ref, cdecp_ref, None, None, *weights, *outs,
                    sp_out, cp_out, ubuf, s_scr)

    @pl.when(i >= n_prompt_tiles)
    def _():
        _mixer_body(nseg, seg, None, xs_ref, coss_ref, sins_ref,
                    masks_ref, qdecs_ref, kdecs_ref, cdecs_ref, s_in, c_in, *weights, *outs,
                    ss_out, cs_out, ubuf, None)


def _retention_tables(seg, nseg, lc):
    T = seg * nseg
    lg = jnp.log(1.0 - 2.0 ** (-5.0 - jnp.arange(RET_HEADS, dtype=F32)))
    i = jnp.arange(T, dtype=jnp.int32)
    loc = (i % seg).astype(F32)
    same_seg = (i[:, None] // seg) == (i[None, :] // seg)
    causal = ((i[None, :] % seg) // lc) <= ((i[:, None] % seg) // lc)
    dist = jnp.abs(loc[:, None] - loc[None, :])
    mask = jnp.where((same_seg & causal)[None], jnp.exp(lg[:, None, None] * dist[None]), 0.0)
    qdec = jnp.exp(lg[:, None] * (loc[None, :] + 1.0))
    kdec = jnp.exp(lg[:, None] * (seg - 1.0 - loc[None, :]))
    cdec = jnp.exp(lg * seg)
    bc = lambda t: jnp.broadcast_to(t[:, :, None], (RET_HEADS, t.shape[1], RET_DK))
    return (mask.astype(F32), bc(qdec), bc(kdec),
            jnp.broadcast_to(cdec[:, None, None], (RET_HEADS, 8, RET_DK)))


def _rope_tables(pos):
    half = RET_DK // 2
    inv = ROPE_BASE ** (-jnp.arange(half, dtype=F32) / half)
    ang = pos[:, None] * inv[None, :]
    cos, sin = jnp.cos(ang), jnp.sin(ang)
    return jnp.concatenate([cos, cos], axis=-1), jnp.concatenate([-sin, sin], axis=-1)


def _const_spec(shape):
    return pl.BlockSpec(shape, lambda *_: (0,) * len(shape))


def _weight_specs():
    return [
        _const_spec((1, D_MODEL)),
        _const_spec((D_MODEL, IN_COLS)),
        _const_spec((CONV_K, CONV_WIDTH)),
        _const_spec((1, CONV_WIDTH)),
        _const_spec((1, CONV_WIDTH)),
        _const_spec((1, CONV_WIDTH)),
        _const_spec((1, RET_WIDTH)),
        _const_spec((D_MODEL, D_MODEL)),
        _const_spec((1, D_MODEL)),
        _const_spec((D_MODEL, N_EXPERTS)),
        _const_spec((1, N_EXPERTS)),
    ]


def _moe_kernel(te_ref, cnt_ref, nv_ref, src0_ref, src1_ref, dst_ref, hn_any, wg_ref, wu_ref, wd_ref,
                bg_ref, bu_ref, bd_ref, y_any, xbuf, ybuf, wg_bf, wu_bf, wd_bf, gsem, ssem):
    j = pl.program_id(0)
    nt = pl.num_programs(0)
    nvalid = nv_ref[0]
    slot = j % 2

    def gather_copy(idx_ref, r, sl):
        return pltpu.make_async_copy(hn_any.at[pl.ds(idx_ref[0, 0, r], 1)],
                                     xbuf.at[sl, pl.ds(r, 1)], gsem.at[sl])

    def scatter_copy(r, sl):
        return pltpu.make_async_copy(ybuf.at[sl, pl.ds(r, 1)],
                                     y_any.at[pl.ds(dst_ref[0, 0, r], 1)], ssem.at[sl])

    def start_gather(idx_ref, sl):
        def body(r, c):
            gather_copy(idx_ref, r, sl).start()
            return c
        lax.fori_loop(0, MOE_TILE, body, 0)

    def wait_gather(sl):
        def body(r, c):
            gather_copy(src0_ref, r, sl).wait()
            return c
        lax.fori_loop(0, MOE_TILE, body, 0)

    def wait_scatter(tile):
        def body(r, c):
            scatter_copy(r, tile % 2).wait()
            return c
        lax.fori_loop(0, cnt_ref[tile], body, 0)

    @pl.when(j == 0)
    def _():
        start_gather(src0_ref, 0)

    @pl.when(j < nvalid)
    def _():
        e_prev = te_ref[jnp.maximum(j - 1, 0)]
        changed = jnp.logical_or(j == 0, e_prev != te_ref[j])

        @pl.when(changed)
        def _():
            wg_bf[...] = wg_ref[0].astype(BF16)
            wu_bf[...] = wu_ref[0].astype(BF16)
            wd_bf[...] = wd_ref[0].astype(BF16)

        @pl.when(j + 1 < nvalid)
        def _():
            start_gather(src1_ref, 1 - slot)

        wait_gather(slot)

        @pl.when(j >= 2)
        def _():
            wait_scatter(j - 2)

        x = xbuf[slot].astype(BF16)
        hg = jnp.minimum(_dot(x, wg_bf[...]) + bg_ref[0], SWIGLU_LIMIT)
        hu = jnp.clip(_dot(x, wu_bf[...]) + bu_ref[0], -SWIGLU_LIMIT, SWIGLU_LIMIT)
        act = (hg * _sigmoid(SWIGLU_ALPHA * hg) * (hu + 1.0)).astype(BF16)
        ybuf[slot] = _dot(act, wd_bf[...]) + bd_ref[0]

        def body(r, c):
            scatter_copy(r, slot).start()
            return c
        lax.fori_loop(0, cnt_ref[j], body, 0)

    @pl.when(j == nt - 1)
    def _():
        wait_scatter(nvalid - 1)

        @pl.when(nvalid >= 2)
        def _():
            wait_scatter(nvalid - 2)


def _combine_kernel(h_ref, y0_ref, y1_ref, y2_ref, y3_ref, w_ref, g_ref, o_ref):
    w = w_ref[...]
    acc = h_ref[...]
    for kk, y_ref in enumerate((y0_ref, y1_ref, y2_ref, y3_ref)):
        acc = acc + w[:, kk:kk + 1] * y_ref[...]
    o_ref[...] = _rms(acc, g_ref[...])


def kernel(x_prompt, x_sample, state_ret, cache_conv, norm_mix_g, w_in, conv_w, conv_b, conv_ln_g,
           conv_ln_b, ret_norm_g, w_out, norm_ffn_g, w_router, b_router, w_e_gate, b_e_gate, w_e_up,
           b_e_up, w_e_down, b_e_down, norm_final_g):
    bp, lp, _ = x_prompt.shape
    bs, ls, _ = x_sample.shape
    depth = norm_mix_g.shape[0]
    assert depth == 1 and lp % TILE == 0 and TILE % ls == 0 and (bs * ls) % TILE == 0
    np_tok, ns_tok = bp * lp, bs * ls
    n_tok = np_tok + ns_tok
    nseg = TILE // ls

    row = lambda t: t.reshape(1, -1)
    weights = (row(norm_mix_g[0]), w_in[0].astype(BF16), conv_w[0], row(conv_b[0]), row(conv_ln_g[0]),
               row(conv_ln_b[0]), row(ret_norm_g[0]), w_out[0].astype(BF16), row(norm_ffn_g[0]),
               w_router[0], row(b_router[0]))

    cos_p, sin_p = _rope_tables(jnp.arange(lp, dtype=F32))
    cos_s, sin_s = _rope_tables(jnp.arange(ls, dtype=F32) + PAST_LEN)
    cos_s, sin_s = jnp.tile(cos_s, (nseg, 1)), jnp.tile(sin_s, (nseg, 1))
    tabs_p = _retention_tables(TILE, 1, CHUNK)
    tabs_s = _retention_tables(ls, nseg, min(CHUNK, ls))
    tiles_seq = lp // TILE
    n_ptiles = np_tok // TILE
    n_stiles = ns_tok // TILE
    p_tile = lambda i: jnp.minimum(i, n_ptiles - 1)
    s_tile = lambda i: jnp.maximum(i - n_ptiles, 0)
    tok_spec = lambda w: pl.BlockSpec((TILE, w), lambda i: (i, 0))
    tab_specs = [_const_spec((RET_HEADS, TILE, TILE)), _const_spec((RET_HEADS, TILE, RET_DK)),
                 _const_spec((RET_HEADS, TILE, RET_DK)), _const_spec((RET_HEADS, 8, RET_DK))]
    state_s_spec = pl.BlockSpec((1, nseg, RET_HEADS, RET_DK, RET_DK), lambda i: (0, s_tile(i), 0, 0, 0))
    cache_s_spec = pl.BlockSpec((1, nseg, CONV_K - 1, CONV_WIDTH), lambda i: (0, s_tile(i), 0, 0))
    h_all, hn_all, idx_all, wts_all, state_p, cache_p, state_s, cache_s = pl.pallas_call(
        functools.partial(_mixer_kernel, n_ptiles, tiles_seq, nseg, ls),
        grid=(n_ptiles + n_stiles,),
        in_specs=[pl.BlockSpec((TILE, D_MODEL), lambda i: (p_tile(i), 0)),
                  pl.BlockSpec((TILE, D_MODEL), lambda i: (s_tile(i), 0)),
                  pl.BlockSpec((TILE, RET_DK), lambda i: (p_tile(i) % tiles_seq, 0)),
                  pl.BlockSpec((TILE, RET_DK), lambda i: (p_tile(i) % tiles_seq, 0)),
                  _const_spec((TILE, RET_DK)), _const_spec((TILE, RET_DK))]
                 + tab_specs + tab_specs + [state_s_spec, cache_s_spec] + _weight_specs(),
        out_specs=[tok_spec(D_MODEL), tok_spec(D_MODEL), tok_spec(TOP_K), tok_spec(TOP_K),
                   pl.BlockSpec((1, 1, RET_HEADS, RET_DK, RET_DK),
                                lambda i: (0, p_tile(i) // tiles_seq, 0, 0, 0)),
                   pl.BlockSpec((1, 1, CONV_K - 1, CONV_WIDTH),
                                lambda i: (0, p_tile(i) // tiles_seq, 0, 0)),
                   state_s_spec, cache_s_spec],
        out_shape=[jax.ShapeDtypeStruct((n_tok, D_MODEL), F32),
                   jax.ShapeDtypeStruct((n_tok, D_MODEL), F32),
                   jax.ShapeDtypeStruct((n_tok, TOP_K), jnp.int32),
                   jax.ShapeDtypeStruct((n_tok, TOP_K), F32),
                   jax.ShapeDtypeStruct((1, bp, RET_HEADS, RET_DK, RET_DK), F32),
                   jax.ShapeDtypeStruct((1, bp, CONV_K - 1, CONV_WIDTH), F32),
                   jax.ShapeDtypeStruct(state_ret.shape, F32),
                   jax.ShapeDtypeStruct(cache_conv.shape, F32)],
        scratch_shapes=[pltpu.VMEM((CONV_PAD + TILE, CONV_WIDTH), F32),
                        pltpu.VMEM((RET_HEADS, RET_DK, RET_DK), F32)],
        compiler_params=pltpu.CompilerParams(dimension_semantics=("arbitrary",),
                                             vmem_limit_bytes=VMEM_LIMIT),
        name="mixer",
    )(x_prompt.reshape(np_tok, D_MODEL), x_sample.reshape(ns_tok, D_MODEL), cos_p, sin_p, cos_s, sin_s,
      *tabs_p, *tabs_s, state_ret, cache_conv, *weights)

    n_pair = n_tok * TOP_K
    n_tiles = n_pair // MOE_TILE + N_EXPERTS
    n_slot = n_tiles * MOE_TILE
    e_flat = idx_all.reshape(n_pair)
    onehot = (e_flat[:, None] == jnp.arange(N_EXPERTS, dtype=jnp.int32)[None, :]).astype(jnp.int32)
    csum = jnp.cumsum(onehot, axis=0)
    rank = jnp.sum(csum * onehot, axis=1) - 1
    counts = csum[-1]
    tiles_e = (counts + MOE_TILE - 1) // MOE_TILE
    tile_end = jnp.cumsum(tiles_e)
    n_valid = tile_end[-1]
    tile_start = tile_end - tiles_e
    pos = (tile_start * MOE_TILE)[e_flat] + rank
    pair = jnp.arange(n_pair, dtype=jnp.int32)
    slot_src = jnp.zeros((n_slot,), jnp.int32).at[pos].set(pair // TOP_K, unique_indices=True)
    slot_dst = jnp.zeros((n_slot,), jnp.int32).at[pos].set(
        (pair % TOP_K) * n_tok + pair // TOP_K, unique_indices=True)
    tile_id = jnp.arange(n_tiles, dtype=jnp.int32)
    tile_e = jnp.searchsorted(tile_end, jnp.minimum(tile_id, n_valid - 1), side="right")
    tile_e = jnp.minimum(tile_e, N_EXPERTS - 1).astype(jnp.int32)
    tile_cnt = jnp.clip(counts[tile_e] - (tile_id - tile_start[tile_e]) * MOE_TILE, 0, MOE_TILE)
    tile_cnt = jnp.where(tile_id < n_valid, tile_cnt, 0).astype(jnp.int32)
    slot_src = slot_src.reshape(n_tiles, 1, MOE_TILE)
    slot_dst = slot_dst.reshape(n_tiles, 1, MOE_TILE)

    any_spec = pl.BlockSpec(memory_space=pl.ANY)
    smem_spec = lambda fn: pl.BlockSpec((1, 1, MOE_TILE), fn, memory_space=pltpu.SMEM)
    w_spec = pl.BlockSpec((1, D_MODEL, D_MODEL), lambda j, te, cnt, nv: (te[j], 0, 0))
    b_spec = pl.BlockSpec((1, 1, D_MODEL), lambda j, te, cnt, nv: (te[j], 0, 0))
    y_rows = pl.pallas_call(
        _moe_kernel,
        grid_spec=pltpu.PrefetchScalarGridSpec(
            num_scalar_prefetch=3,
            grid=(n_tiles,),
            in_specs=[smem_spec(lambda j, te, cnt, nv: (j, 0, 0)),
                      smem_spec(lambda j, te, cnt, nv: (jnp.minimum(j + 1, n_tiles - 1), 0, 0)),
                      smem_spec(lambda j, te, cnt, nv: (j, 0, 0)),
                      any_spec, w_spec, w_spec, w_spec, b_spec, b_spec, b_spec],
            out_specs=any_spec,
            scratch_shapes=[pltpu.VMEM((2, MOE_TILE, D_MODEL), F32),
                            pltpu.VMEM((2, MOE_TILE, D_MODEL), F32),
                            pltpu.VMEM((D_MODEL, D_MODEL), BF16),
                            pltpu.VMEM((D_MODEL, D_MODEL), BF16),
                            pltpu.VMEM((D_MODEL, D_MODEL), BF16),
                            pltpu.SemaphoreType.DMA((2,)),
                            pltpu.SemaphoreType.DMA((2,))]),
        out_shape=jax.ShapeDtypeStruct((n_pair, D_MODEL), F32),
        compiler_params=pltpu.CompilerParams(dimension_semantics=("arbitrary",),
                                             vmem_limit_bytes=VMEM_LIMIT),
        name="moe_experts",
    )(tile_e, tile_cnt, n_valid.reshape(1).astype(jnp.int32), slot_src, slot_src, slot_dst, hn_all,
      w_e_gate[0], w_e_up[0], w_e_down[0],
      b_e_gate[0].reshape(N_EXPERTS, 1, D_MODEL), b_e_up[0].reshape(N_EXPERTS, 1, D_MODEL),
      b_e_down[0].reshape(N_EXPERTS, 1, D_MODEL))

    def combine(row0, rows):
        blk0 = row0 // TILE
        planes = [pl.BlockSpec((TILE, D_MODEL), functools.partial(
            lambda i, kk: (kk * (n_tok // TILE) + blk0 + i, 0), kk=kk)) for kk in range(TOP_K)]
        return pl.pallas_call(
            _combine_kernel,
            grid=(rows // TILE,),
            in_specs=[pl.BlockSpec((TILE, D_MODEL), lambda i: (blk0 + i, 0))] + planes
                     + [pl.BlockSpec((TILE, TOP_K), lambda i: (blk0 + i, 0)), _const_spec((1, D_MODEL))],
            out_specs=pl.BlockSpec((TILE, D_MODEL), lambda i: (i, 0)),
            out_shape=jax.ShapeDtypeStruct((rows, D_MODEL), F32),
            compiler_params=pltpu.CompilerParams(dimension_semantics=("arbitrary",),
                                                 vmem_limit_bytes=VMEM_LIMIT),
            name="combine",
        )(h_all, y_rows, y_rows, y_rows, y_rows, wts_all, row(norm_final_g))

    y_prompt = combine(0, np_tok).reshape(bp, lp, D_MODEL)
    y_sample = combine(np_tok, ns_tok).reshape(bs, ls, D_MODEL)
    return (y_prompt, y_sample, state_p, cache_p, state_s, cache_s)
```

```python
import functools

import jax
import jax.numpy as jnp
from jax import lax
from jax.experimental import pallas as pl
from jax.experimental.pallas import tpu as pltpu

D_MODEL = 1024
CHUNK = 64
RET_WIDTH = 512
RET_HEADS = 4
RET_DK = 128
CONV_WIDTH = 512
CONV_K = 31
IN_COLS = 4 * RET_WIDTH + 2 * CONV_WIDTH
ROPE_BASE = 10000.0
N_EXPERTS = 32
TOP_K = 4
SWIGLU_LIMIT = 7.0
SWIGLU_ALPHA = 1.702
EPS = 1e-6
PAST_LEN = 2048

LANES = 128
SUBLANES = 8
TILE = 256
CONV_PAD = 32
CONV_ROWS = 32
MOE_TILE = 256
DMA_UNROLL = 8
VMEM_LIMIT = 52 * 1024 * 1024

F32 = jnp.float32
BF16 = jnp.bfloat16


def _dot(a, b):
    return jnp.dot(a, b, preferred_element_type=F32)


def _dot_nt(a, b):
    return lax.dot_general(a, b, (((1,), (1,)), ((), ())), preferred_element_type=F32)


def _dot_tn(a, b):
    return lax.dot_general(a, b, (((0,), (0,)), ((), ())), preferred_element_type=F32)


def _rms(x, g):
    return x * lax.rsqrt(jnp.mean(x * x, axis=-1, keepdims=True) + EPS) * g


def _sigmoid(x):
    return 1.0 / (1.0 + jnp.exp(-x))


TOKEN_ROWS = D_MODEL // LANES
assert TOKEN_ROWS == SUBLANES


def _store_token_major(ref, val):
    t = val.shape[0]
    for c in range(TOKEN_ROWS):
        ref[pl.ds(c, t, stride=TOKEN_ROWS), :] = val[:, c * LANES:(c + 1) * LANES]


def _load_token_major(ref, t):
    return jnp.concatenate([ref[pl.ds(c, t, stride=TOKEN_ROWS), :] for c in range(TOKEN_ROWS)], axis=-1)


def _mixer_body(nseg, seg, first, x_ref, cos_ref, sin_ref, mask_ref, qdec_ref, kdec_ref, cdec_ref,
                s_in, c_in, gmix_ref, win_ref, convw_ref, convb_ref, lng_ref, lnb_ref, retg_ref,
                wout_ref, gffn_ref, wr_ref, br_ref,
                h_ref, hn_ref, idx_ref, wts_ref, s_out, c_out, ubuf, s_scr):
    carry = first is not None
    T = nseg * seg
    x = x_ref[...]
    xn = _rms(x, gmix_ref[...]).astype(BF16)
    proj = _dot(xn, win_ref[...])
    cosf = cos_ref[...]
    sinf = sin_ref[...]

    if carry:
        @pl.when(first)
        def _():
            s_scr[...] = jnp.zeros_like(s_scr)
            ubuf[0:CONV_PAD, :] = jnp.zeros((CONV_PAD, CONV_WIDTH), F32)

    o_heads = []
    for h in range(RET_HEADS):
        q = proj[:, h * RET_DK:(h + 1) * RET_DK]
        k = proj[:, RET_WIDTH + h * RET_DK:RET_WIDTH + (h + 1) * RET_DK]
        v = proj[:, 2 * RET_WIDTH + h * RET_DK:2 * RET_WIDTH + (h + 1) * RET_DK].astype(BF16)
        q = q * cosf + pltpu.roll(q, RET_DK // 2, 1) * sinf
        k = (k * cosf + pltpu.roll(k, RET_DK // 2, 1) * sinf) * (RET_DK ** -0.5)
        p = (_dot_nt(q.astype(BF16), k.astype(BF16)) * mask_ref[h]).astype(BF16)
        o = _dot(p, v)
        qd = (q * qdec_ref[h]).astype(BF16)
        kd = (k * kdec_ref[h]).astype(BF16)
        cdec = cdec_ref[h, 0:1, :]
        o_cross = []
        for s in range(nseg):
            rows = slice(s * seg, (s + 1) * seg)
            s_prev = s_scr[h] if carry else s_in[0, s, h]
            o_cross.append(_dot(qd[rows], s_prev.astype(BF16)))
            s_new = cdec * s_prev + _dot_tn(kd[rows], v[rows])
            if carry:
                s_scr[h] = s_new
                s_out[0, 0, h] = s_new
            else:
                s_out[0, s, h] = s_new
        o = o + (o_cross[0] if nseg == 1 else jnp.concatenate(o_cross, axis=0))
        o = o * lax.rsqrt(jnp.mean(o * o, axis=-1, keepdims=True) + EPS)
        g = proj[:, 3 * RET_WIDTH + h * RET_DK:3 * RET_WIDTH + (h + 1) * RET_DK]
        o_heads.append(o * retg_ref[:, h * RET_DK:(h + 1) * RET_DK] * (g * _sigmoid(g)))

    a = proj[:, 4 * RET_WIDTH:4 * RET_WIDTH + CONV_WIDTH]
    ga = proj[:, 4 * RET_WIDTH + CONV_WIDTH:]
    u = a * _sigmoid(ga)
    hist = CONV_PAD - (CONV_K - 1)
    cv_chunks = []
    for s in range(nseg):
        if not carry:
            ubuf[hist:CONV_PAD, :] = c_in[0, s]
        ubuf[CONV_PAD:CONV_PAD + seg, :] = u[s * seg:(s + 1) * seg]
        for c in range(seg // CONV_ROWS):
            acc = jnp.broadcast_to(convb_ref[...], (CONV_ROWS, CONV_WIDTH))
            for j in range(CONV_K):
                r0 = hist + j + c * CONV_ROWS
                acc = acc + convw_ref[j:j + 1, :] * ubuf[r0:r0 + CONV_ROWS, :]
            cv_chunks.append(acc)
        tail = ubuf[hist + seg:CONV_PAD + seg, :]
        if carry:
            c_out[0, 0] = tail
            ubuf[0:CONV_PAD, :] = ubuf[seg:seg + CONV_PAD, :]
        else:
            c_out[0, s] = tail
    cv = jnp.concatenate(cv_chunks, axis=0)
    mu = jnp.mean(cv, axis=-1, keepdims=True)
    xc = cv - mu
    ln = xc * lax.rsqrt(jnp.mean(xc * xc, axis=-1, keepdims=True) + EPS) * lng_ref[...] + lnb_ref[...]
    o_conv = ln * _sigmoid(ln)

    mixed = jnp.concatenate(o_heads + [o_conv], axis=-1).astype(BF16)
    hres = x + _dot(mixed, wout_ref[...])
    h_ref[...] = hres

    hn = _rms(hres, gffn_ref[...])
    _store_token_major(hn_ref, hn)
    hn_hi = hn.astype(BF16)
    hn_lo = (hn - hn_hi.astype(F32)).astype(BF16)
    r_hi = _dot(hn_hi, wr_ref[...])
    r_lo = _dot(hn_lo, wr_ref[...])
    logits = (r_hi[:, :N_EXPERTS] + r_hi[:, N_EXPERTS:2 * N_EXPERTS] + r_lo[:, :N_EXPERTS]
              + br_ref[...])
    lane = lax.broadcasted_iota(jnp.int32, (T, N_EXPERTS), 1).astype(F32)
    col = lax.broadcasted_iota(jnp.int32, (T, TOP_K), 1)
    idx_out = jnp.zeros((T, TOP_K), F32)
    val_out = jnp.zeros((T, TOP_K), F32)
    work = logits
    for kk in range(TOP_K):
        m = jnp.max(work, axis=-1, keepdims=True)
        sel = jnp.min(jnp.where(work == m, lane, float(N_EXPERTS)), axis=-1, keepdims=True)
        idx_out = jnp.where(col == kk, sel, idx_out)
        val_out = jnp.where(col == kk, m, val_out)
        work = jnp.where(lane == sel, -jnp.inf, work)
    ex = jnp.exp(val_out - val_out[:, 0:1])
    wts_ref[...] = ex / jnp.sum(ex, axis=-1, keepdims=True)
    idx_ref[...] = idx_out.astype(jnp.int32)


def _mixer_kernel(n_prompt_tiles, tiles_per_seq, nseg, seg,
                  xp_ref, xs_ref, cosp_ref, sinp_ref, coss_ref, sins_ref,
                  maskp_ref, qdecp_ref, kdecp_ref, cdecp_ref,
                  masks_ref, qdecs_ref, kdecs_ref, cdecs_ref, s_in, c_in,
                  gmix_ref, win_ref, convw_ref, convb_ref, lng_ref, lnb_ref, retg_ref,
                  wout_ref, gffn_ref, wr_ref, br_ref,
                  h_ref, hn_ref, idx_ref, wts_ref, sp_out, cp_out, ss_out, cs_out, ubuf, s_scr):
    i = pl.program_id(0)
    weights = (gmix_ref, win_ref, convw_ref, convb_ref, lng_ref, lnb_ref, retg_ref,
               wout_ref, gffn_ref, wr_ref, br_ref)
    outs = (h_ref, hn_ref, idx_ref, wts_ref)

    @pl.when(i < n_prompt_tiles)
    def _():
        _mixer_body(1, TILE, i % tiles_per_seq == 0, xp_ref, cosp_ref, sinp_ref,
                    maskp_ref, qdecp_ref, kdecp_ref, cdecp_ref, None, None, *weights, *outs,
                    sp_out, cp_out, ubuf, s_scr)

    @pl.when(i >= n_prompt_tiles)
    def _():
        _mixer_body(nseg, seg, None, xs_ref, coss_ref, sins_ref,
                    masks_ref, qdecs_ref, kdecs_ref, cdecs_ref, s_in, c_in, *weights, *outs,
                    ss_out, cs_out, ubuf, None)


def _retention_tables(seg, nseg, lc):
    T = seg * nseg
    lg = jnp.log(1.0 - 2.0 ** (-5.0 - jnp.arange(RET_HEADS, dtype=F32)))
    i = jnp.arange(T, dtype=jnp.int32)
    loc = (i % seg).astype(F32)
    same_seg = (i[:, None] // seg) == (i[None, :] // seg)
    causal = ((i[None, :] % seg) // lc) <= ((i[:, None] % seg) // lc)
    dist = jnp.abs(loc[:, None] - loc[None, :])
    mask = jnp.where((same_seg & causal)[None], jnp.exp(lg[:, None, None] * dist[None]), 0.0)
    qdec = jnp.exp(lg[:, None] * (loc[None, :] + 1.0))
    kdec = jnp.exp(lg[:, None] * (seg - 1.0 - loc[None, :]))
    cdec = jnp.exp(lg * seg)
    bc = lambda t: jnp.broadcast_to(t[:, :, None], (RET_HEADS, t.shape[1], RET_DK))
    return (mask.astype(F32), bc(qdec), bc(kdec),
            jnp.broadcast_to(cdec[:, None, None], (RET_HEADS, 8, RET_DK)))


def _rope_tables(pos):
    half = RET_DK // 2
    inv = ROPE_BASE ** (-jnp.arange(half, dtype=F32) / half)
    ang = pos[:, None] * inv[None, :]
    cos, sin = jnp.cos(ang), jnp.sin(ang)
    return jnp.concatenate([cos, cos], axis=-1), jnp.concatenate([-sin, sin], axis=-1)


def _const_spec(shape):
    return pl.BlockSpec(shape, lambda *_: (0,) * len(shape))


def _weight_specs():
    return [
        _const_spec((1, D_MODEL)),
        _const_spec((D_MODEL, IN_COLS)),
        _const_spec((CONV_K, CONV_WIDTH)),
        _const_spec((1, CONV_WIDTH)),
        _const_spec((1, CONV_WIDTH)),
        _const_spec((1, CONV_WIDTH)),
        _const_spec((1, RET_WIDTH)),
        _const_spec((D_MODEL, D_MODEL)),
        _const_spec((1, D_MODEL)),
        _const_spec((D_MODEL, LANES)),
        _const_spec((1, N_EXPERTS)),
    ]


def _moe_kernel(te_ref, cnt_ref, nv_ref, src0_ref, src1_ref, dst_ref, hn_any, wg_ref, wu_ref, wd_ref,
                bg_ref, bu_ref, bd_ref, y_any, xbuf, ybuf, wg_bf, wu_bf, wd_bf, gsem, ssem):
    j = pl.program_id(0)
    nt = pl.num_programs(0)
    nvalid = nv_ref[0]
    slot = j % 2

    def token_rows(base, n=1):
        if not isinstance(base, int):
            base = pl.multiple_of(base, TOKEN_ROWS)
        return pl.ds(base, n * TOKEN_ROWS)

    def gather_copy(idx_ref, r, sl):
        return pltpu.make_async_copy(hn_any.at[token_rows(idx_ref[0, 0, r])],
                                     xbuf.at[sl, token_rows(r * TOKEN_ROWS)], gsem.at[sl])

    def scatter_copy(r, sl):
        return pltpu.make_async_copy(ybuf.at[sl, token_rows(r * TOKEN_ROWS)],
                                     y_any.at[token_rows(dst_ref[0, 0, r])], ssem.at[sl])

    def start_gather(idx_ref, sl):
        def body(r, c):
            gather_copy(idx_ref, r, sl).start()
            return c
        lax.fori_loop(0, MOE_TILE, body, 0, unroll=DMA_UNROLL)

    def wait_gather(sl):
        pltpu.make_async_copy(hn_any.at[token_rows(0, MOE_TILE)], xbuf.at[sl], gsem.at[sl]).wait()

    def start_scatter(tile):
        n = cnt_ref[tile]
        groups = n // DMA_UNROLL

        def group(g, c):
            for u in range(DMA_UNROLL):
                scatter_copy(g * DMA_UNROLL + u, tile % 2).start()
            return c
        lax.fori_loop(0, groups, group, 0)

        def single(r, c):
            scatter_copy(r, tile % 2).start()
            return c
        lax.fori_loop(groups * DMA_UNROLL, n, single, 0)

    def wait_scatter(tile):
        n = cnt_ref[tile]

        @pl.when(n > 0)
        def _():
            pltpu.make_async_copy(ybuf.at[tile % 2, token_rows(0, n)], y_any.at[token_rows(0, n)],
                                  ssem.at[tile % 2]).wait()

    @pl.when(j == 0)
    def _():
        start_gather(src0_ref, 0)

    @pl.when(j < nvalid)
    def _():
        e_prev = te_ref[jnp.maximum(j - 1, 0)]
        changed = jnp.logical_or(j == 0, e_prev != te_ref[j])

        @pl.when(changed)
        def _():
            wg_bf[...] = wg_ref[0].astype(BF16)
            wu_bf[...] = wu_ref[0].astype(BF16)
            wd_bf[...] = wd_ref[0].astype(BF16)

        @pl.when(j + 1 < nvalid)
        def _():
            start_gather(src1_ref, 1 - slot)

        wait_gather(slot)

        @pl.when(j >= 2)
        def _():
            wait_scatter(j - 2)

        x = _load_token_major(xbuf.at[slot], MOE_TILE).astype(BF16)
        hg = jnp.minimum(_dot(x, wg_bf[...]) + bg_ref[0], SWIGLU_LIMIT)
        hu = jnp.clip(_dot(x, wu_bf[...]) + bu_ref[0], -SWIGLU_LIMIT, SWIGLU_LIMIT)
        act = (hg * _sigmoid(SWIGLU_ALPHA * hg) * (hu + 1.0)).astype(BF16)
        _store_token_major(ybuf.at[slot], _dot(act, wd_bf[...]) + bd_ref[0])
        start_scatter(j)

    @pl.when(j == nt - 1)
    def _():
        wait_scatter(nvalid - 1)

        @pl.when(nvalid >= 2)
        def _():
            wait_scatter(nvalid - 2)


def _combine_kernel(h_ref, y0_ref, y1_ref, y2_ref, y3_ref, w_ref, g_ref, o_ref):
    w = w_ref[...]
    acc = h_ref[...]
    for kk, y_ref in enumerate((y0_ref, y1_ref, y2_ref, y3_ref)):
        acc = acc + w[:, kk:kk + 1] * _load_token_major(y_ref, TILE)
    o_ref[...] = _rms(acc, g_ref[...])


def kernel(x_prompt, x_sample, state_ret, cache_conv, norm_mix_g, w_in, conv_w, conv_b, conv_ln_g,
           conv_ln_b, ret_norm_g, w_out, norm_ffn_g, w_router, b_router, w_e_gate, b_e_gate, w_e_up,
           b_e_up, w_e_down, b_e_down, norm_final_g):
    bp, lp, _ = x_prompt.shape
    bs, ls, _ = x_sample.shape
    depth = norm_mix_g.shape[0]
    assert depth == 1 and lp % TILE == 0 and TILE % ls == 0 and (bs * ls) % TILE == 0
    np_tok, ns_tok = bp * lp, bs * ls
    n_tok = np_tok + ns_tok
    nseg = TILE // ls

    row = lambda t: t.reshape(1, -1)
    wr_hi = w_router[0].astype(BF16)
    wr_lo = (w_router[0] - wr_hi.astype(F32)).astype(BF16)
    wr_cat = jnp.concatenate(
        [wr_hi, wr_lo, jnp.zeros((D_MODEL, LANES - 2 * N_EXPERTS), BF16)], axis=1)
    weights = (row(norm_mix_g[0]), w_in[0].astype(BF16), conv_w[0], row(conv_b[0]), row(conv_ln_g[0]),
               row(conv_ln_b[0]), row(ret_norm_g[0]), w_out[0].astype(BF16), row(norm_ffn_g[0]),
               wr_cat, row(b_router[0]))

    cos_p, sin_p = _rope_tables(jnp.arange(lp, dtype=F32))
    cos_s, sin_s = _rope_tables(jnp.arange(ls, dtype=F32) + PAST_LEN)
    cos_s, sin_s = jnp.tile(cos_s, (nseg, 1)), jnp.tile(sin_s, (nseg, 1))
    tabs_p = _retention_tables(TILE, 1, CHUNK)
    tabs_s = _retention_tables(ls, nseg, min(CHUNK, ls))
    tiles_seq = lp // TILE
    n_ptiles = np_tok // TILE
    n_stiles = ns_tok // TILE
    p_tile = lambda i: jnp.minimum(i, n_ptiles - 1)
    s_tile = lambda i: jnp.maximum(i - n_ptiles, 0)
    tok_spec = lambda w: pl.BlockSpec((TILE, w), lambda i: (i, 0))
    tab_specs = [_const_spec((RET_HEADS, TILE, TILE)), _const_spec((RET_HEADS, TILE, RET_DK)),
                 _const_spec((RET_HEADS, TILE, RET_DK)), _const_spec((RET_HEADS, 8, RET_DK))]
    state_s_spec = pl.BlockSpec((1, nseg, RET_HEADS, RET_DK, RET_DK), lambda i: (0, s_tile(i), 0, 0, 0))
    cache_s_spec = pl.BlockSpec((1, nseg, CONV_K - 1, CONV_WIDTH), lambda i: (0, s_tile(i), 0, 0))
    h_all, hn_all, idx_all, wts_all, state_p, cache_p, state_s, cache_s = pl.pallas_call(
        functools.partial(_mixer_kernel, n_ptiles, tiles_seq, nseg, ls),
        grid=(n_ptiles + n_stiles,),
        in_specs=[pl.BlockSpec((TILE, D_MODEL), lambda i: (p_tile(i), 0)),
                  pl.BlockSpec((TILE, D_MODEL), lambda i: (s_tile(i), 0)),
                  pl.BlockSpec((TILE, RET_DK), lambda i: (p_tile(i) % tiles_seq, 0)),
                  pl.BlockSpec((TILE, RET_DK), lambda i: (p_tile(i) % tiles_seq, 0)),
                  _const_spec((TILE, RET_DK)), _const_spec((TILE, RET_DK))]
                 + tab_specs + tab_specs + [state_s_spec, cache_s_spec] + _weight_specs(),
        out_specs=[tok_spec(D_MODEL), pl.BlockSpec((TILE * TOKEN_ROWS, LANES), lambda i: (i, 0)),
                   tok_spec(TOP_K), tok_spec(TOP_K),
                   pl.BlockSpec((1, 1, RET_HEADS, RET_DK, RET_DK),
                                lambda i: (0, p_tile(i) // tiles_seq, 0, 0, 0)),
                   pl.BlockSpec((1, 1, CONV_K - 1, CONV_WIDTH),
                                lambda i: (0, p_tile(i) // tiles_seq, 0, 0)),
                   state_s_spec, cache_s_spec],
        out_shape=[jax.ShapeDtypeStruct((n_tok, D_MODEL), F32),
                   jax.ShapeDtypeStruct((n_tok * TOKEN_ROWS, LANES), F32),
                   jax.ShapeDtypeStruct((n_tok, TOP_K), jnp.int32),
                   jax.ShapeDtypeStruct((n_tok, TOP_K), F32),
                   jax.ShapeDtypeStruct((1, bp, RET_HEADS, RET_DK, RET_DK), F32),
                   jax.ShapeDtypeStruct((1, bp, CONV_K - 1, CONV_WIDTH), F32),
                   jax.ShapeDtypeStruct(state_ret.shape, F32),
                   jax.ShapeDtypeStruct(cache_conv.shape, F32)],
        scratch_shapes=[pltpu.VMEM((CONV_PAD + TILE, CONV_WIDTH), F32),
                        pltpu.VMEM((RET_HEADS, RET_DK, RET_DK), F32)],
        compiler_params=pltpu.CompilerParams(dimension_semantics=("arbitrary",),
                                             vmem_limit_bytes=VMEM_LIMIT),
        name="mixer",
    )(x_prompt.reshape(np_tok, D_MODEL), x_sample.reshape(ns_tok, D_MODEL), cos_p, sin_p, cos_s, sin_s,
      *tabs_p, *tabs_s, state_ret, cache_conv, *weights)

    n_pair = n_tok * TOP_K
    n_tiles = n_pair // MOE_TILE + N_EXPERTS
    n_slot = n_tiles * MOE_TILE
    e_flat = idx_all.reshape(n_pair)
    onehot = (e_flat[:, None] == jnp.arange(N_EXPERTS, dtype=jnp.int32)[None, :]).astype(jnp.int32)
    csum = jnp.cumsum(onehot, axis=0)
    rank = jnp.sum(csum * onehot, axis=1) - 1
    counts = csum[-1]
    tiles_e = (counts + MOE_TILE - 1) // MOE_TILE
    tile_end = jnp.cumsum(tiles_e)
    n_valid = tile_end[-1]
    tile_start = tile_end - tiles_e
    pos = (tile_start * MOE_TILE)[e_flat] + rank
    pair = jnp.arange(n_pair, dtype=jnp.int32)
    slot_pair = jnp.zeros((n_slot,), jnp.int32).at[pos].set(pair, unique_indices=True)
    slot_src = (slot_pair // TOP_K) * TOKEN_ROWS
    slot_dst = ((slot_pair % TOP_K) * n_tok + slot_pair // TOP_K) * TOKEN_ROWS
    tile_id = jnp.arange(n_tiles, dtype=jnp.int32)
    tile_q = jnp.minimum(tile_id, n_valid - 1)
    tile_e = jnp.sum((tile_end[None, :] <= tile_q[:, None]).astype(jnp.int32), axis=1)
    tile_e = jnp.minimum(tile_e, N_EXPERTS - 1).astype(jnp.int32)
    tile_cnt = jnp.clip(counts[tile_e] - (tile_id - tile_start[tile_e]) * MOE_TILE, 0, MOE_TILE)
    tile_cnt = jnp.where(tile_id < n_valid, tile_cnt, 0).astype(jnp.int32)
    slot_src = slot_src.reshape(n_tiles, 1, MOE_TILE)
    slot_dst = slot_dst.reshape(n_tiles, 1, MOE_TILE)

    any_spec = pl.BlockSpec(memory_space=pl.ANY)
    smem_spec = lambda fn: pl.BlockSpec((1, 1, MOE_TILE), fn, memory_space=pltpu.SMEM)
    w_spec = pl.BlockSpec((1, D_MODEL, D_MODEL), lambda j, te, cnt, nv: (te[j], 0, 0))
    b_spec = pl.BlockSpec((1, 1, D_MODEL), lambda j, te, cnt, nv: (te[j], 0, 0))
    y_rows = pl.pallas_call(
        _moe_kernel,
        grid_spec=pltpu.PrefetchScalarGridSpec(
            num_scalar_prefetch=3,
            grid=(n_tiles,),
            in_specs=[smem_spec(lambda j, te, cnt, nv: (j, 0, 0)),
                      smem_spec(lambda j, te, cnt, nv: (jnp.minimum(j + 1, n_tiles - 1), 0, 0)),
                      smem_spec(lambda j, te, cnt, nv: (j, 0, 0)),
                      any_spec, w_spec, w_spec, w_spec, b_spec, b_spec, b_spec],
            out_specs=any_spec,
            scratch_shapes=[pltpu.VMEM((2, MOE_TILE * TOKEN_ROWS, LANES), F32),
                            pltpu.VMEM((2, MOE_TILE * TOKEN_ROWS, LANES), F32),
                            pltpu.VMEM((D_MODEL, D_MODEL), BF16),
                            pltpu.VMEM((D_MODEL, D_MODEL), BF16),
                            pltpu.VMEM((D_MODEL, D_MODEL), BF16),
                            pltpu.SemaphoreType.DMA((2,)),
                            pltpu.SemaphoreType.DMA((2,))]),
        out_shape=jax.ShapeDtypeStruct((n_pair * TOKEN_ROWS, LANES), F32),
        compiler_params=pltpu.CompilerParams(dimension_semantics=("arbitrary",),
                                             vmem_limit_bytes=VMEM_LIMIT),
        name="moe_experts",
    )(tile_e, tile_cnt, n_valid.reshape(1).astype(jnp.int32), slot_src, slot_src, slot_dst, hn_all,
      w_e_gate[0], w_e_up[0], w_e_down[0],
      b_e_gate[0].reshape(N_EXPERTS, 1, D_MODEL), b_e_up[0].reshape(N_EXPERTS, 1, D_MODEL),
      b_e_down[0].reshape(N_EXPERTS, 1, D_MODEL))

    def combine(row0, rows):
        blk0 = row0 // TILE
        planes = [pl.BlockSpec((TILE * TOKEN_ROWS, LANES), functools.partial(
            lambda i, kk: (kk * (n_tok // TILE) + blk0 + i, 0), kk=kk)) for kk in range(TOP_K)]
        return pl.pallas_call(
            _combine_kernel,
            grid=(rows // TILE,),
            in_specs=[pl.BlockSpec((TILE, D_MODEL), lambda i: (blk0 + i, 0))] + planes
                     + [pl.BlockSpec((TILE, TOP_K), lambda i: (blk0 + i, 0)), _const_spec((1, D_MODEL))],
            out_specs=pl.BlockSpec((TILE, D_MODEL), lambda i: (i, 0)),
            out_shape=jax.ShapeDtypeStruct((rows, D_MODEL), F32),
            compiler_params=pltpu.CompilerParams(dimension_semantics=("arbitrary",),
                                                 vmem_limit_bytes=VMEM_LIMIT),
            name="combine",
        )(h_all, y_rows, y_rows, y_rows, y_rows, wts_all, row(norm_final_g))

    y_prompt = combine(0, np_tok).reshape(bp, lp, D_MODEL)
    y_sample = combine(np_tok, ns_tok).reshape(bs, ls, D_MODEL)
    return (y_prompt, y_sample, state_p, cache_p, state_s, cache_s)
```

```python
import functools

import jax
import jax.numpy as jnp
from jax import lax
from jax.experimental import pallas as pl
from jax.experimental.pallas import tpu as pltpu

D_MODEL = 1024
CHUNK = 64
RET_WIDTH = 512
RET_HEADS = 4
RET_DK = 128
CONV_WIDTH = 512
CONV_K = 31
IN_COLS = 4 * RET_WIDTH + 2 * CONV_WIDTH
ROPE_BASE = 10000.0
N_EXPERTS = 32
TOP_K = 4
SWIGLU_LIMIT = 7.0
SWIGLU_ALPHA = 1.702
EPS = 1e-6
PAST_LEN = 2048

LANES = 128
SUBLANES = 8
TILE = 256
CONV_PAD = 32
CONV_ROWS = 32
MOE_TILE = 256
DMA_UNROLL = 8
FFN_CHUNKS = 4
VMEM_LIMIT = 52 * 1024 * 1024

F32 = jnp.float32
BF16 = jnp.bfloat16


def _dot(a, b):
    return jnp.dot(a, b, preferred_element_type=F32)


def _dot_nt(a, b):
    return lax.dot_general(a, b, (((1,), (1,)), ((), ())), preferred_element_type=F32)


def _dot_tn(a, b):
    return lax.dot_general(a, b, (((0,), (0,)), ((), ())), preferred_element_type=F32)


def _rms(x, g):
    return x * lax.rsqrt(jnp.mean(x * x, axis=-1, keepdims=True) + EPS) * g


def _sigmoid(x):
    return 1.0 / (1.0 + jnp.exp(-x))


TOKEN_ROWS = D_MODEL // LANES
assert TOKEN_ROWS == SUBLANES


def _store_token_major(ref, val):
    t = val.shape[0]
    for c in range(TOKEN_ROWS):
        ref[pl.ds(c, t, stride=TOKEN_ROWS), :] = val[:, c * LANES:(c + 1) * LANES]


def _load_token_major(ref, t):
    return jnp.concatenate([ref[pl.ds(c, t, stride=TOKEN_ROWS), :] for c in range(TOKEN_ROWS)], axis=-1)


def _mixer_body(nseg, seg, first, x_ref, cos_ref, sin_ref, mask_ref, qdec_ref, kdec_ref, cdec_ref,
                s_in, c_in, gmix_ref, win_ref, convw_ref, convb_ref, lng_ref, lnb_ref, retg_ref,
                wout_ref, gffn_ref, wr_ref, br_ref,
                h_ref, hn_ref, idx_ref, wts_ref, s_out, c_out, ubuf, s_scr):
    carry = first is not None
    T = nseg * seg
    x = x_ref[...]
    xn = _rms(x, gmix_ref[...]).astype(BF16)
    proj = _dot(xn, win_ref[...])
    cosf = cos_ref[...]
    sinf = sin_ref[...]

    if carry:
        @pl.when(first)
        def _():
            s_scr[...] = jnp.zeros_like(s_scr)
            ubuf[0:CONV_PAD, :] = jnp.zeros((CONV_PAD, CONV_WIDTH), F32)

    o_heads = []
    for h in range(RET_HEADS):
        q = proj[:, h * RET_DK:(h + 1) * RET_DK]
        k = proj[:, RET_WIDTH + h * RET_DK:RET_WIDTH + (h + 1) * RET_DK]
        v = proj[:, 2 * RET_WIDTH + h * RET_DK:2 * RET_WIDTH + (h + 1) * RET_DK].astype(BF16)
        q = q * cosf + pltpu.roll(q, RET_DK // 2, 1) * sinf
        k = (k * cosf + pltpu.roll(k, RET_DK // 2, 1) * sinf) * (RET_DK ** -0.5)
        p = (_dot_nt(q.astype(BF16), k.astype(BF16)) * mask_ref[h]).astype(BF16)
        o = _dot(p, v)
        qd = (q * qdec_ref[h]).astype(BF16)
        kd = (k * kdec_ref[h]).astype(BF16)
        cdec = cdec_ref[h, 0:1, :]
        o_cross = []
        for s in range(nseg):
            rows = slice(s * seg, (s + 1) * seg)
            s_prev = s_scr[h] if carry else s_in[0, s, h]
            o_cross.append(_dot(qd[rows], s_prev.astype(BF16)))
            s_new = cdec * s_prev + _dot_tn(kd[rows], v[rows])
            if carry:
                s_scr[h] = s_new
                s_out[0, 0, h] = s_new
            else:
                s_out[0, s, h] = s_new
        o = o + (o_cross[0] if nseg == 1 else jnp.concatenate(o_cross, axis=0))
        o = o * lax.rsqrt(jnp.mean(o * o, axis=-1, keepdims=True) + EPS)
        g = proj[:, 3 * RET_WIDTH + h * RET_DK:3 * RET_WIDTH + (h + 1) * RET_DK]
        o_heads.append(o * retg_ref[:, h * RET_DK:(h + 1) * RET_DK] * (g * _sigmoid(g)))

    a = proj[:, 4 * RET_WIDTH:4 * RET_WIDTH + CONV_WIDTH]
    ga = proj[:, 4 * RET_WIDTH + CONV_WIDTH:]
    u = a * _sigmoid(ga)
    hist = CONV_PAD - (CONV_K - 1)
    cv_chunks = []
    for s in range(nseg):
        if not carry:
            ubuf[hist:CONV_PAD, :] = c_in[0, s]
        ubuf[CONV_PAD:CONV_PAD + seg, :] = u[s * seg:(s + 1) * seg]
        for c in range(seg // CONV_ROWS):
            acc = jnp.broadcast_to(convb_ref[...], (CONV_ROWS, CONV_WIDTH))
            for shift in range(SUBLANES):
                rows = CONV_ROWS + (SUBLANES if shift else 0)
                part = None
                for j in range(CONV_K):
                    if (hist + j) % SUBLANES == shift:
                        r0 = c * CONV_ROWS + (hist + j) - shift
                        term = convw_ref[j:j + 1, :] * ubuf[r0:r0 + rows, :]
                        part = term if part is None else part + term
                acc = acc + part[shift:shift + CONV_ROWS, :]
            cv_chunks.append(acc)
        tail = ubuf[hist + seg:CONV_PAD + seg, :]
        if carry:
            c_out[0, 0] = tail
            ubuf[0:CONV_PAD, :] = ubuf[seg:seg + CONV_PAD, :]
        else:
            c_out[0, s] = tail
    cv = jnp.concatenate(cv_chunks, axis=0)
    mu = jnp.mean(cv, axis=-1, keepdims=True)
    xc = cv - mu
    ln = xc * lax.rsqrt(jnp.mean(xc * xc, axis=-1, keepdims=True) + EPS) * lng_ref[...] + lnb_ref[...]
    o_conv = ln * _sigmoid(ln)

    mixed = jnp.concatenate(o_heads + [o_conv], axis=-1).astype(BF16)
    hres = x + _dot(mixed, wout_ref[...])
    h_ref[...] = hres

    hn = _rms(hres, gffn_ref[...])
    _store_token_major(hn_ref, hn)
    hn_hi = hn.astype(BF16)
    hn_lo = (hn - hn_hi.astype(F32)).astype(BF16)
    r_hi = _dot(hn_hi, wr_ref[...])
    r_lo = _dot(hn_lo, wr_ref[...])
    logits = (r_hi[:, :N_EXPERTS] + r_hi[:, N_EXPERTS:2 * N_EXPERTS] + r_lo[:, :N_EXPERTS]
              + br_ref[...])
    lane = lax.broadcasted_iota(jnp.int32, (T, N_EXPERTS), 1).astype(F32)
    col = lax.broadcasted_iota(jnp.int32, (T, TOP_K), 1)
    idx_out = jnp.zeros((T, TOP_K), F32)
    val_out = jnp.zeros((T, TOP_K), F32)
    work = logits
    for kk in range(TOP_K):
        m = jnp.max(work, axis=-1, keepdims=True)
        sel = jnp.min(jnp.where(work == m, lane, float(N_EXPERTS)), axis=-1, keepdims=True)
        idx_out = jnp.where(col == kk, sel, idx_out)
        val_out = jnp.where(col == kk, m, val_out)
        work = jnp.where(lane == sel, -jnp.inf, work)
    ex = jnp.exp(val_out - val_out[:, 0:1])
    wts_ref[...] = ex / jnp.sum(ex, axis=-1, keepdims=True)
    idx_ref[...] = idx_out.astype(jnp.int32)


def _mixer_kernel(n_prompt_tiles, tiles_per_seq, nseg, seg,
                  xp_ref, xs_ref, cosp_ref, sinp_ref, coss_ref, sins_ref,
                  maskp_ref, qdecp_ref, kdecp_ref, cdecp_ref,
                  masks_ref, qdecs_ref, kdecs_ref, cdecs_ref, s_in, c_in,
                  gmix_ref, win_ref, convw_ref, convb_ref, lng_ref, lnb_ref, retg_ref,
                  wout_ref, gffn_ref, wr_ref, br_ref,
                  h_ref, hn_ref, idx_ref, wts_ref, sp_out, cp_out, ss_out, cs_out, ubuf, s_scr):
    i = pl.program_id(0)
    weights = (gmix_ref, win_ref, convw_ref, convb_ref, lng_ref, lnb_ref, retg_ref,
               wout_ref, gffn_ref, wr_ref, br_ref)
    outs = (h_ref, hn_ref, idx_ref, wts_ref)

    @pl.when(i < n_prompt_tiles)
    def _():
        _mixer_body(1, TILE, i % tiles_per_seq == 0, xp_ref, cosp_ref, sinp_ref,
                    maskp_ref, qdecp_ref, kdecp_ref, cdecp_ref, None, None, *weights, *outs,
                    sp_out, cp_out, ubuf, s_scr)

    @pl.when(i >= n_prompt_tiles)
    def _():
        _mixer_body(nseg, seg, None, xs_ref, coss_ref, sins_ref,
                    masks_ref, qdecs_ref, kdecs_ref, cdecs_ref, s_in, c_in, *weights, *outs,
                    ss_out, cs_out, ubuf, None)


def _retention_tables(seg, nseg, lc):
    T = seg * nseg
    lg = jnp.log(1.0 - 2.0 ** (-5.0 - jnp.arange(RET_HEADS, dtype=F32)))
    i = jnp.arange(T, dtype=jnp.int32)
    loc = (i % seg).astype(F32)
    same_seg = (i[:, None] // seg) == (i[None, :] // seg)
    causal = ((i[None, :] % seg) // lc) <= ((i[:, None] % seg) // lc)
    dist = jnp.abs(loc[:, None] - loc[None, :])
    mask = jnp.where((same_seg & causal)[None], jnp.exp(lg[:, None, None] * dist[None]), 0.0)
    qdec = jnp.exp(lg[:, None] * (loc[None, :] + 1.0))
    kdec = jnp.exp(lg[:, None] * (seg - 1.0 - loc[None, :]))
    cdec = jnp.exp(lg * seg)
    bc = lambda t: jnp.broadcast_to(t[:, :, None], (RET_HEADS, t.shape[1], RET_DK))
    return (mask.astype(F32), bc(qdec), bc(kdec),
            jnp.broadcast_to(cdec[:, None, None], (RET_HEADS, 8, RET_DK)))


def _rope_tables(pos):
    half = RET_DK // 2
    inv = ROPE_BASE ** (-jnp.arange(half, dtype=F32) / half)
    ang = pos[:, None] * inv[None, :]
    cos, sin = jnp.cos(ang), jnp.sin(ang)
    return jnp.concatenate([cos, cos], axis=-1), jnp.concatenate([-sin, sin], axis=-1)


def _const_spec(shape):
    return pl.BlockSpec(shape, lambda *_: (0,) * len(shape))


def _weight_specs():
    return [
        _const_spec((1, D_MODEL)),
        _const_spec((D_MODEL, IN_COLS)),
        _const_spec((CONV_K, CONV_WIDTH)),
        _const_spec((1, CONV_WIDTH)),
        _const_spec((1, CONV_WIDTH)),
        _const_spec((1, CONV_WIDTH)),
        _const_spec((1, RET_WIDTH)),
        _const_spec((D_MODEL, D_MODEL)),
        _const_spec((1, D_MODEL)),
        _const_spec((D_MODEL, LANES)),
        _const_spec((1, N_EXPERTS)),
    ]


def _moe_kernel(te_ref, cnt_ref, nv_ref, src0_ref, src1_ref, dst_ref, hn_any, wg_ref, wu_ref, wd_ref,
                bg_ref, bu_ref, bd_ref, y_any, xbuf, ybuf, wg_bf, wu_bf, wd_bf, gsem, ssem):
    j = pl.program_id(0)
    nt = pl.num_programs(0)
    nvalid = nv_ref[0]
    slot = j % 2

    def token_rows(base, n=1):
        if not isinstance(base, int):
            base = pl.multiple_of(base, TOKEN_ROWS)
        return pl.ds(base, n * TOKEN_ROWS)

    def gather_copy(idx_ref, r, sl):
        return pltpu.make_async_copy(hn_any.at[token_rows(idx_ref[0, 0, r])],
                                     xbuf.at[sl, token_rows(r * TOKEN_ROWS)], gsem.at[sl])

    def scatter_copy(r, sl):
        return pltpu.make_async_copy(ybuf.at[sl, token_rows(r * TOKEN_ROWS)],
                                     y_any.at[token_rows(dst_ref[0, 0, r])], ssem.at[sl])

    def start_gather(idx_ref, sl):
        def body(r, c):
            gather_copy(idx_ref, r, sl).start()
            return c
        lax.fori_loop(0, MOE_TILE, body, 0, unroll=DMA_UNROLL)

    def wait_gather(sl):
        pltpu.make_async_copy(hn_any.at[token_rows(0, MOE_TILE)], xbuf.at[sl], gsem.at[sl]).wait()

    def start_scatter(tile):
        n = cnt_ref[tile]
        groups = n // DMA_UNROLL

        def group(g, c):
            for u in range(DMA_UNROLL):
                scatter_copy(g * DMA_UNROLL + u, tile % 2).start()
            return c
        lax.fori_loop(0, groups, group, 0)

        def single(r, c):
            scatter_copy(r, tile % 2).start()
            return c
        lax.fori_loop(groups * DMA_UNROLL, n, single, 0)

    def wait_scatter(tile):
        n = cnt_ref[tile]
        pltpu.make_async_copy(ybuf.at[tile % 2, token_rows(0, n)], y_any.at[token_rows(0, n)],
                              ssem.at[tile % 2]).wait()

    @pl.when(j == 0)
    def _():
        start_gather(src0_ref, 0)

    def compute(before_store, dma_starts=()):
        starts = list(dma_starts)
        per_chunk = -(-len(starts) // (3 * FFN_CHUNKS))
        width = D_MODEL // FFN_CHUNKS

        def issue_some():
            for _ in range(min(per_chunk, len(starts))):
                starts.pop(0)()

        x = _load_token_major(xbuf.at[slot], MOE_TILE).astype(BF16)
        gate, act, out = [], [], []
        for c in range(FFN_CHUNKS):
            cols = slice(c * width, (c + 1) * width)
            gate.append(jnp.minimum(_dot(x, wg_bf[:, cols]) + bg_ref[0, :, cols], SWIGLU_LIMIT))
            issue_some()
        for c in range(FFN_CHUNKS):
            cols = slice(c * width, (c + 1) * width)
            up = jnp.clip(_dot(x, wu_bf[:, cols]) + bu_ref[0, :, cols], -SWIGLU_LIMIT, SWIGLU_LIMIT)
            act.append((gate[c] * _sigmoid(SWIGLU_ALPHA * gate[c]) * (up + 1.0)).astype(BF16))
            issue_some()
        act = jnp.concatenate(act, axis=-1)
        for c in range(FFN_CHUNKS):
            cols = slice(c * width, (c + 1) * width)
            out.append(_dot(act, wd_bf[:, cols]) + bd_ref[0, :, cols])
            issue_some()
        assert not starts
        before_store()
        _store_token_major(ybuf.at[slot], jnp.concatenate(out, axis=-1))

    @pl.when(j < nvalid)
    def _():
        e_prev = te_ref[jnp.maximum(j - 1, 0)]
        changed = jnp.logical_or(j == 0, e_prev != te_ref[j])

        @pl.when(changed)
        def _():
            wg_bf[...] = wg_ref[0].astype(BF16)
            wu_bf[...] = wu_ref[0].astype(BF16)
            wd_bf[...] = wd_ref[0].astype(BF16)

        wait_gather(slot)
        steady = jnp.logical_and(jnp.logical_and(j >= 2, j + 1 < nvalid),
                                 cnt_ref[jnp.maximum(j - 1, 0)] == MOE_TILE)

        @pl.when(steady)
        def _():
            starts = []
            for r in range(MOE_TILE):
                starts.append(gather_copy(src1_ref, r, 1 - slot).start)
                starts.append(functools.partial(scatter_copy(r, 1 - slot).start, priority=1))
            compute(lambda: wait_scatter(j - 2), starts)

        @pl.when(jnp.logical_not(steady))
        def _():
            @pl.when(j + 1 < nvalid)
            def _():
                start_gather(src1_ref, 1 - slot)

            @pl.when(j >= 1)
            def _():
                start_scatter(j - 1)

            def retire():
                @pl.when(j >= 2)
                def _():
                    wait_scatter(j - 2)
            compute(retire)

    @pl.when(j == nvalid)
    def _():
        start_scatter(j - 1)

    @pl.when(j == nt - 1)
    def _():
        wait_scatter(nvalid - 1)

        @pl.when(nvalid >= 2)
        def _():
            wait_scatter(nvalid - 2)


def _combine_kernel(h_ref, y0_ref, y1_ref, y2_ref, y3_ref, w_ref, g_ref, o_ref):
    w = w_ref[...]
    acc = h_ref[...]
    for kk, y_ref in enumerate((y0_ref, y1_ref, y2_ref, y3_ref)):
        acc = acc + w[:, kk:kk + 1] * _load_token_major(y_ref, TILE)
    o_ref[...] = _rms(acc, g_ref[...])


def kernel(x_prompt, x_sample, state_ret, cache_conv, norm_mix_g, w_in, conv_w, conv_b, conv_ln_g,
           conv_ln_b, ret_norm_g, w_out, norm_ffn_g, w_router, b_router, w_e_gate, b_e_gate, w_e_up,
           b_e_up, w_e_down, b_e_down, norm_final_g):
    bp, lp, _ = x_prompt.shape
    bs, ls, _ = x_sample.shape
    depth = norm_mix_g.shape[0]
    assert depth == 1 and lp % TILE == 0 and TILE % ls == 0 and (bs * ls) % TILE == 0
    np_tok, ns_tok = bp * lp, bs * ls
    n_tok = np_tok + ns_tok
    nseg = TILE // ls

    row = lambda t: t.reshape(1, -1)
    wr_hi = w_router[0].astype(BF16)
    wr_lo = (w_router[0] - wr_hi.astype(F32)).astype(BF16)
    wr_cat = jnp.concatenate(
        [wr_hi, wr_lo, jnp.zeros((D_MODEL, LANES - 2 * N_EXPERTS), BF16)], axis=1)
    weights = (row(norm_mix_g[0]), w_in[0].astype(BF16), conv_w[0], row(conv_b[0]), row(conv_ln_g[0]),
               row(conv_ln_b[0]), row(ret_norm_g[0]), w_out[0].astype(BF16), row(norm_ffn_g[0]),
               wr_cat, row(b_router[0]))

    cos_p, sin_p = _rope_tables(jnp.arange(lp, dtype=F32))
    cos_s, sin_s = _rope_tables(jnp.arange(ls, dtype=F32) + PAST_LEN)
    cos_s, sin_s = jnp.tile(cos_s, (nseg, 1)), jnp.tile(sin_s, (nseg, 1))
    tabs_p = _retention_tables(TILE, 1, CHUNK)
    tabs_s = _retention_tables(ls, nseg, min(CHUNK, ls))
    tiles_seq = lp // TILE
    n_ptiles = np_tok // TILE
    n_stiles = ns_tok // TILE
    p_tile = lambda i: jnp.minimum(i, n_ptiles - 1)
    s_tile = lambda i: jnp.maximum(i - n_ptiles, 0)
    tok_spec = lambda w: pl.BlockSpec((TILE, w), lambda i: (i, 0))
    tab_specs = [_const_spec((RET_HEADS, TILE, TILE)), _const_spec((RET_HEADS, TILE, RET_DK)),
                 _const_spec((RET_HEADS, TILE, RET_DK)), _const_spec((RET_HEADS, 8, RET_DK))]
    state_s_spec = pl.BlockSpec((1, nseg, RET_HEADS, RET_DK, RET_DK), lambda i: (0, s_tile(i), 0, 0, 0))
    cache_s_spec = pl.BlockSpec((1, nseg, CONV_K - 1, CONV_WIDTH), lambda i: (0, s_tile(i), 0, 0))
    h_all, hn_all, idx_all, wts_all, state_p, cache_p, state_s, cache_s = pl.pallas_call(
        functools.partial(_mixer_kernel, n_ptiles, tiles_seq, nseg, ls),
        grid=(n_ptiles + n_stiles,),
        in_specs=[pl.BlockSpec((TILE, D_MODEL), lambda i: (p_tile(i), 0)),
                  pl.BlockSpec((TILE, D_MODEL), lambda i: (s_tile(i), 0)),
                  pl.BlockSpec((TILE, RET_DK), lambda i: (p_tile(i) % tiles_seq, 0)),
                  pl.BlockSpec((TILE, RET_DK), lambda i: (p_tile(i) % tiles_seq, 0)),
                  _const_spec((TILE, RET_DK)), _const_spec((TILE, RET_DK))]
                 + tab_specs + tab_specs + [state_s_spec, cache_s_spec] + _weight_specs(),
        out_specs=[tok_spec(D_MODEL), pl.BlockSpec((TILE * TOKEN_ROWS, LANES), lambda i: (i, 0)),
                   tok_spec(TOP_K), tok_spec(TOP_K),
                   pl.BlockSpec((1, 1, RET_HEADS, RET_DK, RET_DK),
                                lambda i: (0, p_tile(i) // tiles_seq, 0, 0, 0)),
                   pl.BlockSpec((1, 1, CONV_K - 1, CONV_WIDTH),
                                lambda i: (0, p_tile(i) // tiles_seq, 0, 0)),
                   state_s_spec, cache_s_spec],
        out_shape=[jax.ShapeDtypeStruct((n_tok, D_MODEL), F32),
                   jax.ShapeDtypeStruct((n_tok * TOKEN_ROWS, LANES), F32),
                   jax.ShapeDtypeStruct((n_tok, TOP_K), jnp.int32),
                   jax.ShapeDtypeStruct((n_tok, TOP_K), F32),
                   jax.ShapeDtypeStruct((1, bp, RET_HEADS, RET_DK, RET_DK), F32),
                   jax.ShapeDtypeStruct((1, bp, CONV_K - 1, CONV_WIDTH), F32),
                   jax.ShapeDtypeStruct(state_ret.shape, F32),
                   jax.ShapeDtypeStruct(cache_conv.shape, F32)],
        scratch_shapes=[pltpu.VMEM((CONV_PAD + TILE, CONV_WIDTH), F32),
                        pltpu.VMEM((RET_HEADS, RET_DK, RET_DK), F32)],
        compiler_params=pltpu.CompilerParams(dimension_semantics=("arbitrary",),
                                             vmem_limit_bytes=VMEM_LIMIT),
        name="mixer",
    )(x_prompt.reshape(np_tok, D_MODEL), x_sample.reshape(ns_tok, D_MODEL), cos_p, sin_p, cos_s, sin_s,
      *tabs_p, *tabs_s, state_ret, cache_conv, *weights)

    n_pair = n_tok * TOP_K
    n_tiles = n_pair // MOE_TILE + N_EXPERTS
    n_slot = n_tiles * MOE_TILE
    e_flat = idx_all.reshape(n_pair)
    onehot = (e_flat[:, None] == jnp.arange(N_EXPERTS, dtype=jnp.int32)[None, :]).astype(jnp.int32)
    csum = jnp.cumsum(onehot, axis=0)
    rank = jnp.sum(csum * onehot, axis=1) - 1
    counts = csum[-1]
    tiles_e = (counts + MOE_TILE - 1) // MOE_TILE
    tile_end = jnp.cumsum(tiles_e)
    n_valid = tile_end[-1]
    tile_start = tile_end - tiles_e
    pos = (tile_start * MOE_TILE)[e_flat] + rank
    pair = jnp.arange(n_pair, dtype=jnp.int32)
    slot_pair = jnp.zeros((n_slot,), jnp.int32).at[pos].set(pair, unique_indices=True)
    slot_src = (slot_pair // TOP_K) * TOKEN_ROWS
    slot_dst = ((slot_pair % TOP_K) * n_tok + slot_pair // TOP_K) * TOKEN_ROWS
    tile_id = jnp.arange(n_tiles, dtype=jnp.int32)
    tile_q = jnp.minimum(tile_id, n_valid - 1)
    tile_e = jnp.sum((tile_end[None, :] <= tile_q[:, None]).astype(jnp.int32), axis=1)
    tile_e = jnp.minimum(tile_e, N_EXPERTS - 1).astype(jnp.int32)
    tile_cnt = jnp.clip(counts[tile_e] - (tile_id - tile_start[tile_e]) * MOE_TILE, 0, MOE_TILE)
    tile_cnt = jnp.where(tile_id < n_valid, tile_cnt, 0).astype(jnp.int32)
    slot_src = slot_src.reshape(n_tiles, 1, MOE_TILE)
    slot_dst = slot_dst.reshape(n_tiles, 1, MOE_TILE)

    any_spec = pl.BlockSpec(memory_space=pl.ANY)
    smem_spec = lambda fn: pl.BlockSpec((1, 1, MOE_TILE), fn, memory_space=pltpu.SMEM)
    w_spec = pl.BlockSpec((1, D_MODEL, D_MODEL), lambda j, te, cnt, nv: (te[j], 0, 0))
    b_spec = pl.BlockSpec((1, 1, D_MODEL), lambda j, te, cnt, nv: (te[j], 0, 0))
    y_rows = pl.pallas_call(
        _moe_kernel,
        grid_spec=pltpu.PrefetchScalarGridSpec(
            num_scalar_prefetch=3,
            grid=(n_tiles,),
            in_specs=[smem_spec(lambda j, te, cnt, nv: (j, 0, 0)),
                      smem_spec(lambda j, te, cnt, nv: (jnp.minimum(j + 1, n_tiles - 1), 0, 0)),
                      smem_spec(lambda j, te, cnt, nv: (jnp.maximum(j - 1, 0), 0, 0)),
                      any_spec, w_spec, w_spec, w_spec, b_spec, b_spec, b_spec],
            out_specs=any_spec,
            scratch_shapes=[pltpu.VMEM((2, MOE_TILE * TOKEN_ROWS, LANES), F32),
                            pltpu.VMEM((2, MOE_TILE * TOKEN_ROWS, LANES), F32),
                            pltpu.VMEM((D_MODEL, D_MODEL), BF16),
                            pltpu.VMEM((D_MODEL, D_MODEL), BF16),
                            pltpu.VMEM((D_MODEL, D_MODEL), BF16),
                            pltpu.SemaphoreType.DMA((2,)),
                            pltpu.SemaphoreType.DMA((2,))]),
        out_shape=jax.ShapeDtypeStruct((n_pair * TOKEN_ROWS, LANES), F32),
        compiler_params=pltpu.CompilerParams(dimension_semantics=("arbitrary",),
                                             vmem_limit_bytes=VMEM_LIMIT),
        name="moe_experts",
    )(tile_e, tile_cnt, n_valid.reshape(1).astype(jnp.int32), slot_src, slot_src, slot_dst, hn_all,
      w_e_gate[0], w_e_up[0], w_e_down[0],
      b_e_gate[0].reshape(N_EXPERTS, 1, D_MODEL), b_e_up[0].reshape(N_EXPERTS, 1, D_MODEL),
      b_e_down[0].reshape(N_EXPERTS, 1, D_MODEL))

    def combine(row0, rows):
        blk0 = row0 // TILE
        planes = [pl.BlockSpec((TILE * TOKEN_ROWS, LANES), functools.partial(
            lambda i, kk: (kk * (n_tok // TILE) + blk0 + i, 0), kk=kk)) for kk in range(TOP_K)]
        return pl.pallas_call(
            _combine_kernel,
            grid=(rows // TILE,),
            in_specs=[pl.BlockSpec((TILE, D_MODEL), lambda i: (blk0 + i, 0))] + planes
                     + [pl.BlockSpec((TILE, TOP_K), lambda i: (blk0 + i, 0)), _const_spec((1, D_MODEL))],
            out_specs=pl.BlockSpec((TILE, D_MODEL), lambda i: (i, 0)),
            out_shape=jax.ShapeDtypeStruct((rows, D_MODEL), F32),
            compiler_params=pltpu.CompilerParams(dimension_semantics=("arbitrary",),
                                                 vmem_limit_bytes=VMEM_LIMIT),
            name="combine",
        )(h_all, y_rows, y_rows, y_rows, y_rows, wts_all, row(norm_final_g))

    y_prompt = combine(0, np_tok).reshape(bp, lp, D_MODEL)
    y_sample = combine(np_tok, ns_tok).reshape(bs, ls, D_MODEL)
    return (y_prompt, y_sample, state_p, cache_p, state_s, cache_s)
```

```python
import functools

import jax
import jax.numpy as jnp
from jax import lax
from jax.experimental import pallas as pl
from jax.experimental.pallas import tpu as pltpu

D_MODEL = 1024
CHUNK = 64
RET_WIDTH = 512
RET_HEADS = 4
RET_DK = 128
CONV_WIDTH = 512
CONV_K = 31
IN_COLS = 4 * RET_WIDTH + 2 * CONV_WIDTH
ROPE_BASE = 10000.0
N_EXPERTS = 32
TOP_K = 4
SWIGLU_LIMIT = 7.0
SWIGLU_ALPHA = 1.702
EPS = 1e-6
PAST_LEN = 2048

LANES = 128
SUBLANES = 8
SUB = 256
SUBTILES = 2
TILE = SUB * SUBTILES
COMBINE_TILE = 256
CONV_PAD = 32
CONV_ROWS = 32
MOE_TILE = 256
DMA_UNROLL = 8
FFN_CHUNKS = 4
GATHER_AHEAD = 2
X_BUFFERS = GATHER_AHEAD + 1
VMEM_LIMIT = 52 * 1024 * 1024
MIXER_VMEM_LIMIT = 60 * 1024 * 1024

F32 = jnp.float32
BF16 = jnp.bfloat16


def _dot(a, b):
    return jnp.dot(a, b, preferred_element_type=F32)


def _dot_nt(a, b):
    return lax.dot_general(a, b, (((1,), (1,)), ((), ())), preferred_element_type=F32)


def _dot_tn(a, b):
    return lax.dot_general(a, b, (((0,), (0,)), ((), ())), preferred_element_type=F32)


def _rms(x, g):
    return x * lax.rsqrt(jnp.mean(x * x, axis=-1, keepdims=True) + EPS) * g


def _sigmoid(x):
    return 1.0 / (1.0 + jnp.exp(-x))


TOKEN_ROWS = D_MODEL // LANES
assert TOKEN_ROWS == SUBLANES


def _store_token_major(ref, val):
    t = val.shape[0]
    for c in range(TOKEN_ROWS):
        ref[pl.ds(c, t, stride=TOKEN_ROWS), :] = val[:, c * LANES:(c + 1) * LANES]


def _load_token_major(ref, t):
    return jnp.concatenate([ref[pl.ds(c, t, stride=TOKEN_ROWS), :] for c in range(TOKEN_ROWS)], axis=-1)


def _mixer_body(nseg, seg, first, x_ref, cos_ref, sin_ref, mask_ref, qdec_ref, kdec_ref, cdec_ref,
                s_in, c_in, gmix_ref, win_ref, convw_ref, convb_ref, lng_ref, lnb_ref, retg_ref,
                wout_ref, gffn_ref, wr_ref, br_ref,
                h_ref, hn_ref, idx_ref, wts_ref, cnt_ref, s_out, c_out, ubuf, s_scr,
                seg0=0, ubuf_prev=None):
    carry = first is not None
    T = nseg * seg
    x = x_ref[...]
    xn = _rms(x, gmix_ref[...]).astype(BF16)
    proj = _dot(xn, win_ref[...])
    cosf = cos_ref[...]
    sinf = sin_ref[...]

    if carry:
        def continue_sequence():
            ubuf[0:CONV_PAD, :] = ubuf_prev[seg:seg + CONV_PAD, :]

        if first is False:
            continue_sequence()
        else:
            @pl.when(first)
            def _():
                s_scr[...] = jnp.zeros_like(s_scr)
                ubuf[0:CONV_PAD, :] = jnp.zeros((CONV_PAD, CONV_WIDTH), F32)

            pl.when(jnp.logical_not(first))(continue_sequence)

    o_heads = []
    for h in range(RET_HEADS):
        q = proj[:, h * RET_DK:(h + 1) * RET_DK]
        k = proj[:, RET_WIDTH + h * RET_DK:RET_WIDTH + (h + 1) * RET_DK]
        v = proj[:, 2 * RET_WIDTH + h * RET_DK:2 * RET_WIDTH + (h + 1) * RET_DK].astype(BF16)
        q = q * cosf + pltpu.roll(q, RET_DK // 2, 1) * sinf
        k = (k * cosf + pltpu.roll(k, RET_DK // 2, 1) * sinf) * (RET_DK ** -0.5)
        p = (_dot_nt(q.astype(BF16), k.astype(BF16)) * mask_ref[h]).astype(BF16)
        o = _dot(p, v)
        qd = (q * qdec_ref[h]).astype(BF16)
        kd = (k * kdec_ref[h]).astype(BF16)
        cdec = cdec_ref[h, 0:1, :]
        o_cross = []
        for s in range(nseg):
            rows = slice(s * seg, (s + 1) * seg)
            s_prev = s_scr[h] if carry else s_in[0, seg0 + s, h]
            o_cross.append(_dot(qd[rows], s_prev.astype(BF16)))
            s_new = cdec * s_prev + _dot_tn(kd[rows], v[rows])
            if carry:
                s_scr[h] = s_new
                s_out[0, 0, h] = s_new
            else:
                s_out[0, seg0 + s, h] = s_new
        o = o + (o_cross[0] if nseg == 1 else jnp.concatenate(o_cross, axis=0))
        o = o * lax.rsqrt(jnp.mean(o * o, axis=-1, keepdims=True) + EPS)
        g = proj[:, 3 * RET_WIDTH + h * RET_DK:3 * RET_WIDTH + (h + 1) * RET_DK]
        o_heads.append(o * retg_ref[:, h * RET_DK:(h + 1) * RET_DK] * (g * _sigmoid(g)))

    a = proj[:, 4 * RET_WIDTH:4 * RET_WIDTH + CONV_WIDTH]
    ga = proj[:, 4 * RET_WIDTH + CONV_WIDTH:]
    u = a * _sigmoid(ga)
    hist = CONV_PAD - (CONV_K - 1)
    cv_chunks = []
    for s in range(nseg):
        if not carry:
            ubuf[hist:CONV_PAD, :] = c_in[0, seg0 + s]
        ubuf[CONV_PAD:CONV_PAD + seg, :] = u[s * seg:(s + 1) * seg]
        for c in range(seg // CONV_ROWS):
            acc = jnp.broadcast_to(convb_ref[...], (CONV_ROWS, CONV_WIDTH))
            for shift in range(SUBLANES):
                rows = CONV_ROWS + (SUBLANES if shift else 0)
                part = None
                for j in range(CONV_K):
                    if (hist + j) % SUBLANES == shift:
                        r0 = c * CONV_ROWS + (hist + j) - shift
                        term = convw_ref[j:j + 1, :] * ubuf[r0:r0 + rows, :]
                        part = term if part is None else part + term
                acc = acc + part[shift:shift + CONV_ROWS, :]
            cv_chunks.append(acc)
        tail = ubuf[hist + seg:CONV_PAD + seg, :]
        if carry:
            c_out[0, 0] = tail
        else:
            c_out[0, seg0 + s] = tail
    cv = jnp.concatenate(cv_chunks, axis=0)
    mu = jnp.mean(cv, axis=-1, keepdims=True)
    xc = cv - mu
    ln = xc * lax.rsqrt(jnp.mean(xc * xc, axis=-1, keepdims=True) + EPS) * lng_ref[...] + lnb_ref[...]
    o_conv = ln * _sigmoid(ln)

    mixed = jnp.concatenate(o_heads + [o_conv], axis=-1).astype(BF16)
    hres = x + _dot(mixed, wout_ref[...])
    h_ref[...] = hres

    hn = _rms(hres, gffn_ref[...])
    _store_token_major(hn_ref, hn)
    hn_hi = hn.astype(BF16)
    hn_lo = (hn - hn_hi.astype(F32)).astype(BF16)
    r_hi = _dot_nt(wr_ref[...], hn_hi)
    r_lo = _dot_nt(wr_ref[...], hn_lo)
    logits = (r_hi[:N_EXPERTS] + r_hi[N_EXPERTS:2 * N_EXPERTS] + r_lo[:N_EXPERTS] + br_ref[...])
    expert = lax.broadcasted_iota(jnp.int32, (N_EXPERTS, T), 0).astype(F32)
    work = logits
    top_v, top_i = [], []
    picked = jnp.zeros((N_EXPERTS, T), F32)
    for kk in range(TOP_K):
        m = jnp.max(work, axis=0, keepdims=True)
        sel = jnp.min(jnp.where(work == m, expert, float(N_EXPERTS)), axis=0, keepdims=True)
        top_v.append(m)
        top_i.append(sel)
        hit = expert == sel
        picked = picked + hit.astype(F32)
        work = jnp.where(hit, -jnp.inf, work)
    ex = jnp.exp(jnp.concatenate(top_v, axis=0) - top_v[0])
    wts_ref[...] = ex / jnp.sum(ex, axis=0, keepdims=True)
    idx_ref[...] = jnp.concatenate(top_i, axis=0).astype(jnp.int32)
    cnt_ref[...] += jnp.broadcast_to(jnp.sum(picked, axis=1, keepdims=True), (N_EXPERTS, LANES))


def _mixer_kernel(n_prompt_tiles, tiles_per_seq, nseg, seg,
                  xp_ref, xs_ref, cosp_ref, sinp_ref, coss_ref, sins_ref,
                  maskp_ref, qdecp_ref, kdecp_ref, cdecp_ref,
                  masks_ref, qdecs_ref, kdecs_ref, cdecs_ref, s_in, c_in,
                  gmix_ref, win_ref, convw_ref, convb_ref, lng_ref, lnb_ref, retg_ref,
                  wout_ref, gffn_ref, wr_ref, br_ref,
                  h_ref, hn_ref, idx_ref, wts_ref, cnt_ref, sp_out, cp_out, ss_out, cs_out,
                  ubufs, s_scr):
    i = pl.program_id(0)
    weights = (gmix_ref, win_ref, convw_ref, convb_ref, lng_ref, lnb_ref, retg_ref,
               wout_ref, gffn_ref, wr_ref, br_ref)

    def rows(ref, k, per_row=1):
        return ref.at[pl.ds(k * SUB * per_row, SUB * per_row)]

    def outs(k):
        cols = pl.ds(k * SUB, SUB)
        return (rows(h_ref, k), rows(hn_ref, k, TOKEN_ROWS), idx_ref.at[:, cols], wts_ref.at[:, cols],
                cnt_ref)

    @pl.when(i == 0)
    def _():
        cnt_ref[...] = jnp.zeros_like(cnt_ref)

    @pl.when(i < n_prompt_tiles)
    def _():
        for k in range(SUBTILES):
            first = (i % tiles_per_seq == 0) if k == 0 else False
            _mixer_body(1, SUB, first, rows(xp_ref, k), rows(cosp_ref, k), rows(sinp_ref, k),
                        maskp_ref, qdecp_ref, kdecp_ref, cdecp_ref, None, None, *weights, *outs(k),
                        sp_out, cp_out, ubufs.at[k], s_scr, ubuf_prev=ubufs.at[(k - 1) % SUBTILES])

    @pl.when(i >= n_prompt_tiles)
    def _():
        for k in range(SUBTILES):
            _mixer_body(nseg, seg, None, rows(xs_ref, k), coss_ref, sins_ref,
                        masks_ref, qdecs_ref, kdecs_ref, cdecs_ref, s_in, c_in, *weights, *outs(k),
                        ss_out, cs_out, ubufs.at[k], None, seg0=k * nseg)


def _retention_tables(seg, nseg, lc):
    T = seg * nseg
    lg = jnp.log(1.0 - 2.0 ** (-5.0 - jnp.arange(RET_HEADS, dtype=F32)))
    i = jnp.arange(T, dtype=jnp.int32)
    loc = (i % seg).astype(F32)
    same_seg = (i[:, None] // seg) == (i[None, :] // seg)
    causal = ((i[None, :] % seg) // lc) <= ((i[:, None] % seg) // lc)
    dist = jnp.abs(loc[:, None] - loc[None, :])
    mask = jnp.where((same_seg & causal)[None], jnp.exp(lg[:, None, None] * dist[None]), 0.0)
    qdec = jnp.exp(lg[:, None] * (loc[None, :] + 1.0))
    kdec = jnp.exp(lg[:, None] * (seg - 1.0 - loc[None, :]))
    cdec = jnp.exp(lg * seg)
    bc = lambda t: jnp.broadcast_to(t[:, :, None], (RET_HEADS, t.shape[1], RET_DK))
    return (mask.astype(F32), bc(qdec), bc(kdec),
            jnp.broadcast_to(cdec[:, None, None], (RET_HEADS, 8, RET_DK)))


def _rope_tables(pos):
    half = RET_DK // 2
    inv = ROPE_BASE ** (-jnp.arange(half, dtype=F32) / half)
    ang = pos[:, None] * inv[None, :]
    cos, sin = jnp.cos(ang), jnp.sin(ang)
    return jnp.concatenate([cos, cos], axis=-1), jnp.concatenate([-sin, sin], axis=-1)


def _const_spec(shape):
    return pl.BlockSpec(shape, lambda *_: (0,) * len(shape), pipeline_mode=pl.Buffered(1))


def _weight_specs():
    return [
        _const_spec((1, D_MODEL)),
        _const_spec((D_MODEL, IN_COLS)),
        _const_spec((CONV_K, CONV_WIDTH)),
        _const_spec((1, CONV_WIDTH)),
        _const_spec((1, CONV_WIDTH)),
        _const_spec((1, CONV_WIDTH)),
        _const_spec((1, RET_WIDTH)),
        _const_spec((D_MODEL, D_MODEL)),
        _const_spec((1, D_MODEL)),
        _const_spec((LANES, D_MODEL)),
        _const_spec((N_EXPERTS, SUB)),
    ]


def _moe_kernel(te_ref, cnt_ref, nv_ref, first_ref, ahead_ref, dst_ref, hn_any, wg_ref, wu_ref, wd_ref,
                bg_ref, bu_ref, bd_ref, y_any, xbuf, ybuf, wg_bf, wu_bf, wd_bf, gsem, ssem):
    j = pl.program_id(0)
    nt = pl.num_programs(0)
    nvalid = nv_ref[0]
    slot = j % 2
    xslot = j % X_BUFFERS

    def token_rows(base, n=1):
        if not isinstance(base, int):
            base = pl.multiple_of(base, TOKEN_ROWS)
        return pl.ds(base, n * TOKEN_ROWS)

    def gather_copy(idx_ref, r, sl, t=0):
        return pltpu.make_async_copy(hn_any.at[token_rows(idx_ref[t, 0, r])],
                                     xbuf.at[sl, token_rows(r * TOKEN_ROWS)], gsem.at[sl])

    def scatter_copy(r, sl):
        return pltpu.make_async_copy(ybuf.at[sl, token_rows(r * TOKEN_ROWS)],
                                     y_any.at[token_rows(dst_ref[0, 0, r])], ssem.at[sl])

    def start_gather(idx_ref, sl, t=0):
        def body(r, c):
            gather_copy(idx_ref, r, sl, t).start()
            return c
        lax.fori_loop(0, MOE_TILE, body, 0, unroll=DMA_UNROLL)

    def wait_gather(sl):
        pltpu.make_async_copy(hn_any.at[token_rows(0, MOE_TILE)], xbuf.at[sl], gsem.at[sl]).wait()

    def start_scatter(tile):
        n = cnt_ref[tile]
        groups = n // DMA_UNROLL

        def group(g, c):
            for u in range(DMA_UNROLL):
                scatter_copy(g * DMA_UNROLL + u, tile % 2).start()
            return c
        lax.fori_loop(0, groups, group, 0)

        def single(r, c):
            scatter_copy(r, tile % 2).start()
            return c
        lax.fori_loop(groups * DMA_UNROLL, n, single, 0)

    def wait_scatter(tile):
        n = cnt_ref[tile]
        pltpu.make_async_copy(ybuf.at[tile % 2, token_rows(0, n)], y_any.at[token_rows(0, n)],
                              ssem.at[tile % 2]).wait()

    @pl.when(j == 0)
    def _():
        for t in range(GATHER_AHEAD):
            @pl.when(t < nvalid)
            def _():
                start_gather(first_ref, t, t)

    ahead_slot = (j + GATHER_AHEAD) % X_BUFFERS

    def compute(before_store, dma_starts=()):
        starts = list(dma_starts)
        per_chunk = -(-len(starts) // (3 * FFN_CHUNKS))
        width = D_MODEL // FFN_CHUNKS

        def issue_some():
            for _ in range(min(per_chunk, len(starts))):
                starts.pop(0)()

        x = _load_token_major(xbuf.at[xslot], MOE_TILE).astype(BF16)
        gate, act, out = [], [], []
        for c in range(FFN_CHUNKS):
            cols = slice(c * width, (c + 1) * width)
            gate.append(jnp.minimum(_dot(x, wg_bf[:, cols]) + bg_ref[0, :, cols], SWIGLU_LIMIT))
            issue_some()
        for c in range(FFN_CHUNKS):
            cols = slice(c * width, (c + 1) * width)
            up = jnp.clip(_dot(x, wu_bf[:, cols]) + bu_ref[0, :, cols], -SWIGLU_LIMIT, SWIGLU_LIMIT)
            act.append((gate[c] * _sigmoid(SWIGLU_ALPHA * gate[c]) * (up + 1.0)).astype(BF16))
            issue_some()
        act = jnp.concatenate(act, axis=-1)
        for c in range(FFN_CHUNKS):
            cols = slice(c * width, (c + 1) * width)
            out.append(_dot(act, wd_bf[:, cols]) + bd_ref[0, :, cols])
            issue_some()
        assert not starts
        before_store()
        _store_token_major(ybuf.at[slot], jnp.concatenate(out, axis=-1))

    @pl.when(j < nvalid)
    def _():
        e_prev = te_ref[jnp.maximum(j - 1, 0)]
        changed = jnp.logical_or(j == 0, e_prev != te_ref[j])

        @pl.when(changed)
        def _():
            wg_bf[...] = wg_ref[0].astype(BF16)
            wu_bf[...] = wu_ref[0].astype(BF16)
            wd_bf[...] = wd_ref[0].astype(BF16)

        wait_gather(xslot)
        steady = jnp.logical_and(jnp.logical_and(j >= 2, j + GATHER_AHEAD < nvalid),
                                 cnt_ref[jnp.maximum(j - 1, 0)] == MOE_TILE)

        @pl.when(steady)
        def _():
            starts = []
            for r in range(MOE_TILE):
                starts.append(gather_copy(ahead_ref, r, ahead_slot).start)
                starts.append(functools.partial(scatter_copy(r, 1 - slot).start, priority=1))
            compute(lambda: wait_scatter(j - 2), starts)

        @pl.when(jnp.logical_not(steady))
        def _():
            @pl.when(j + GATHER_AHEAD < nvalid)
            def _():
                start_gather(ahead_ref, ahead_slot)

            @pl.when(j >= 1)
            def _():
                start_scatter(j - 1)

            def retire():
                @pl.when(j >= 2)
                def _():
                    wait_scatter(j - 2)
            compute(retire)

    @pl.when(j == nvalid)
    def _():
        start_scatter(j - 1)

    @pl.when(j == nt - 1)
    def _():
        wait_scatter(nvalid - 1)

        @pl.when(nvalid >= 2)
        def _():
            wait_scatter(nvalid - 2)


def _combine_kernel(h_ref, y0_ref, y1_ref, y2_ref, y3_ref, w_ref, g_ref, o_ref):
    w = w_ref[...]
    acc = h_ref[...]
    for kk, y_ref in enumerate((y0_ref, y1_ref, y2_ref, y3_ref)):
        acc = acc + w[:, kk:kk + 1] * _load_token_major(y_ref, COMBINE_TILE)
    o_ref[...] = _rms(acc, g_ref[...])


def kernel(x_prompt, x_sample, state_ret, cache_conv, norm_mix_g, w_in, conv_w, conv_b, conv_ln_g,
           conv_ln_b, ret_norm_g, w_out, norm_ffn_g, w_router, b_router, w_e_gate, b_e_gate, w_e_up,
           b_e_up, w_e_down, b_e_down, norm_final_g):
    bp, lp, _ = x_prompt.shape
    bs, ls, _ = x_sample.shape
    depth = norm_mix_g.shape[0]
    assert depth == 1 and lp % TILE == 0 and TILE % ls == 0 and (bs * ls) % TILE == 0
    np_tok, ns_tok = bp * lp, bs * ls
    n_tok = np_tok + ns_tok
    nseg = SUB // ls

    row = lambda t: t.reshape(1, -1)
    wr_hi = w_router[0].astype(BF16)
    wr_lo = (w_router[0] - wr_hi.astype(F32)).astype(BF16)
    wr_cat = jnp.concatenate(
        [wr_hi, wr_lo, jnp.zeros((D_MODEL, LANES - 2 * N_EXPERTS), BF16)], axis=1).T
    br_bcast = jnp.broadcast_to(b_router[0][:, None], (N_EXPERTS, SUB))
    weights = (row(norm_mix_g[0]), w_in[0].astype(BF16), conv_w[0], row(conv_b[0]), row(conv_ln_g[0]),
               row(conv_ln_b[0]), row(ret_norm_g[0]), w_out[0].astype(BF16), row(norm_ffn_g[0]),
               wr_cat, br_bcast)

    cos_p, sin_p = _rope_tables(jnp.arange(lp, dtype=F32))
    cos_s, sin_s = _rope_tables(jnp.arange(ls, dtype=F32) + PAST_LEN)
    cos_s, sin_s = jnp.tile(cos_s, (nseg, 1)), jnp.tile(sin_s, (nseg, 1))
    tabs_p = _retention_tables(SUB, 1, CHUNK)
    tabs_s = _retention_tables(ls, nseg, min(CHUNK, ls))
    tiles_seq = lp // TILE
    n_ptiles = np_tok // TILE
    n_stiles = ns_tok // TILE
    p_tile = lambda i: jnp.minimum(i, n_ptiles - 1)
    s_tile = lambda i: jnp.maximum(i - n_ptiles, 0)
    tok_spec = lambda w: pl.BlockSpec((TILE, w), lambda i: (i, 0))
    tab_specs = [_const_spec((RET_HEADS, SUB, SUB)), _const_spec((RET_HEADS, SUB, RET_DK)),
                 _const_spec((RET_HEADS, SUB, RET_DK)), _const_spec((RET_HEADS, 8, RET_DK))]
    tile_seqs = SUBTILES * nseg
    state_s_spec = pl.BlockSpec((1, tile_seqs, RET_HEADS, RET_DK, RET_DK),
                                lambda i: (0, s_tile(i), 0, 0, 0))
    cache_s_spec = pl.BlockSpec((1, tile_seqs, CONV_K - 1, CONV_WIDTH), lambda i: (0, s_tile(i), 0, 0))
    h_all, hn_all, idx_t, wts_t, cnt_all, state_p, cache_p, state_s, cache_s = pl.pallas_call(
        functools.partial(_mixer_kernel, n_ptiles, tiles_seq, nseg, ls),
        grid=(n_ptiles + n_stiles,),
        in_specs=[pl.BlockSpec((TILE, D_MODEL), lambda i: (p_tile(i), 0)),
                  pl.BlockSpec((TILE, D_MODEL), lambda i: (s_tile(i), 0)),
                  pl.BlockSpec((TILE, RET_DK), lambda i: (p_tile(i) % tiles_seq, 0)),
                  pl.BlockSpec((TILE, RET_DK), lambda i: (p_tile(i) % tiles_seq, 0)),
                  _const_spec((SUB, RET_DK)), _const_spec((SUB, RET_DK))]
                 + tab_specs + tab_specs + [state_s_spec, cache_s_spec] + _weight_specs(),
        out_specs=[tok_spec(D_MODEL), pl.BlockSpec((TILE * TOKEN_ROWS, LANES), lambda i: (i, 0)),
                   pl.BlockSpec((TOP_K, TILE), lambda i: (0, i)),
                   pl.BlockSpec((TOP_K, TILE), lambda i: (0, i)),
                   pl.BlockSpec((N_EXPERTS, LANES), lambda i: (0, 0)),
                   pl.BlockSpec((1, 1, RET_HEADS, RET_DK, RET_DK),
                                lambda i: (0, p_tile(i) // tiles_seq, 0, 0, 0)),
                   pl.BlockSpec((1, 1, CONV_K - 1, CONV_WIDTH),
                                lambda i: (0, p_tile(i) // tiles_seq, 0, 0)),
                   state_s_spec, cache_s_spec],
        out_shape=[jax.ShapeDtypeStruct((n_tok, D_MODEL), F32),
                   jax.ShapeDtypeStruct((n_tok * TOKEN_ROWS, LANES), F32),
                   jax.ShapeDtypeStruct((TOP_K, n_tok), jnp.int32),
                   jax.ShapeDtypeStruct((TOP_K, n_tok), F32),
                   jax.ShapeDtypeStruct((N_EXPERTS, LANES), F32),
                   jax.ShapeDtypeStruct((1, bp, RET_HEADS, RET_DK, RET_DK), F32),
                   jax.ShapeDtypeStruct((1, bp, CONV_K - 1, CONV_WIDTH), F32),
                   jax.ShapeDtypeStruct(state_ret.shape, F32),
                   jax.ShapeDtypeStruct(cache_conv.shape, F32)],
        scratch_shapes=[pltpu.VMEM((SUBTILES, CONV_PAD + SUB, CONV_WIDTH), F32),
                        pltpu.VMEM((RET_HEADS, RET_DK, RET_DK), F32)],
        compiler_params=pltpu.CompilerParams(dimension_semantics=("arbitrary",),
                                             vmem_limit_bytes=MIXER_VMEM_LIMIT),
        name="mixer",
    )(x_prompt.reshape(np_tok, D_MODEL), x_sample.reshape(ns_tok, D_MODEL), cos_p, sin_p, cos_s, sin_s,
      *tabs_p, *tabs_s, state_ret, cache_conv, *weights)

    n_pair = n_tok * TOP_K
    n_tiles = n_pair // MOE_TILE + N_EXPERTS
    n_slot = n_tiles * MOE_TILE
    e_flat = idx_t.reshape(n_pair)
    counts = cnt_all[:, 0].astype(jnp.int32)
    tiles_e = (counts + MOE_TILE - 1) // MOE_TILE
    tile_end = jnp.cumsum(tiles_e)
    n_valid = tile_end[-1]
    tile_start = tile_end - tiles_e
    pair_bits = (n_pair - 1).bit_length()
    pad_mark = (1 << pair_bits) - 1
    assert pad_mark >= n_pair and (N_EXPERTS + 1) << pair_bits < 2 ** 31
    pad_end = jnp.cumsum(tiles_e * MOE_TILE - counts)
    pad_id = jnp.arange(n_slot - n_pair, dtype=jnp.int32)
    pad_e = jnp.sum((pad_end[None, :] <= pad_id[:, None]).astype(jnp.int32), axis=1)
    pair = jnp.arange(n_pair, dtype=jnp.int32)
    keys = jnp.concatenate([(e_flat << pair_bits) | pair, (pad_e << pair_bits) | pad_mark])
    slot_pair = jnp.sort(keys) & pad_mark
    slot_pair = jnp.where(slot_pair == pad_mark, 0, slot_pair)
    slot_src = (slot_pair % n_tok) * TOKEN_ROWS
    slot_dst = slot_pair * TOKEN_ROWS
    tile_id = jnp.arange(n_tiles, dtype=jnp.int32)
    tile_q = jnp.minimum(tile_id, n_valid - 1)
    tile_e = jnp.sum((tile_end[None, :] <= tile_q[:, None]).astype(jnp.int32), axis=1)
    tile_e = jnp.minimum(tile_e, N_EXPERTS - 1).astype(jnp.int32)
    tile_cnt = jnp.clip(counts[tile_e] - (tile_id - tile_start[tile_e]) * MOE_TILE, 0, MOE_TILE)
    tile_cnt = jnp.where(tile_id < n_valid, tile_cnt, 0).astype(jnp.int32)
    slot_src = slot_src.reshape(n_tiles, 1, MOE_TILE)
    slot_dst = slot_dst.reshape(n_tiles, 1, MOE_TILE)

    any_spec = pl.BlockSpec(memory_space=pl.ANY)
    smem_spec = lambda fn: pl.BlockSpec((1, 1, MOE_TILE), fn, memory_space=pltpu.SMEM)
    w_spec = pl.BlockSpec((1, D_MODEL, D_MODEL), lambda j, te, cnt, nv: (te[j], 0, 0))
    b_spec = pl.BlockSpec((1, 1, D_MODEL), lambda j, te, cnt, nv: (te[j], 0, 0))
    y_rows = pl.pallas_call(
        _moe_kernel,
        grid_spec=pltpu.PrefetchScalarGridSpec(
            num_scalar_prefetch=3,
            grid=(n_tiles,),
            in_specs=[pl.BlockSpec((GATHER_AHEAD, 1, MOE_TILE), lambda j, te, cnt, nv: (0, 0, 0),
                                   memory_space=pltpu.SMEM),
                      smem_spec(lambda j, te, cnt, nv: (jnp.minimum(j + GATHER_AHEAD, n_tiles - 1), 0, 0)),
                      smem_spec(lambda j, te, cnt, nv: (jnp.maximum(j - 1, 0), 0, 0)),
                      any_spec, w_spec, w_spec, w_spec, b_spec, b_spec, b_spec],
            out_specs=any_spec,
            scratch_shapes=[pltpu.VMEM((X_BUFFERS, MOE_TILE * TOKEN_ROWS, LANES), F32),
                            pltpu.VMEM((2, MOE_TILE * TOKEN_ROWS, LANES), F32),
                            pltpu.VMEM((D_MODEL, D_MODEL), BF16),
                            pltpu.VMEM((D_MODEL, D_MODEL), BF16),
                            pltpu.VMEM((D_MODEL, D_MODEL), BF16),
                            pltpu.SemaphoreType.DMA((X_BUFFERS,)),
                            pltpu.SemaphoreType.DMA((2,))]),
        out_shape=jax.ShapeDtypeStruct((n_pair * TOKEN_ROWS, LANES), F32),
        compiler_params=pltpu.CompilerParams(dimension_semantics=("arbitrary",),
                                             vmem_limit_bytes=VMEM_LIMIT),
        name="moe_experts",
    )(tile_e, tile_cnt, n_valid.reshape(1).astype(jnp.int32), slot_src, slot_src, slot_dst, hn_all,
      w_e_gate[0], w_e_up[0], w_e_down[0],
      b_e_gate[0].reshape(N_EXPERTS, 1, D_MODEL), b_e_up[0].reshape(N_EXPERTS, 1, D_MODEL),
      b_e_down[0].reshape(N_EXPERTS, 1, D_MODEL))

    def combine(row0, rows):
        ct = COMBINE_TILE
        blk0 = row0 // ct
        planes = [pl.BlockSpec((ct * TOKEN_ROWS, LANES), functools.partial(
            lambda i, kk: (kk * (n_tok // ct) + blk0 + i, 0), kk=kk)) for kk in range(TOP_K)]
        return pl.pallas_call(
            _combine_kernel,
            grid=(rows // ct,),
            in_specs=[pl.BlockSpec((ct, D_MODEL), lambda i: (blk0 + i, 0))] + planes
                     + [pl.BlockSpec((ct, TOP_K), lambda i: (blk0 + i, 0)), _const_spec((1, D_MODEL))],
            out_specs=pl.BlockSpec((ct, D_MODEL), lambda i: (i, 0)),
            out_shape=jax.ShapeDtypeStruct((rows, D_MODEL), F32),
            compiler_params=pltpu.CompilerParams(dimension_semantics=("arbitrary",),
                                                 vmem_limit_bytes=VMEM_LIMIT),
            name="combine",
        )(h_all, y_rows, y_rows, y_rows, y_rows, wts_all, row(norm_final_g))

    wts_all = wts_t.T

    y_prompt = combine(0, np_tok).reshape(bp, lp, D_MODEL)
    y_sample = combine(np_tok, ns_tok).reshape(bs, ls, D_MODEL)
    return (y_prompt, y_sample, state_p, cache_p, state_s, cache_s)
```

```python
import functools

import jax
import jax.numpy as jnp
from jax import lax
from jax.experimental import pallas as pl
from jax.experimental.pallas import tpu as pltpu

D_MODEL = 1024
CHUNK = 64
RET_WIDTH = 512
RET_HEADS = 4
RET_DK = 128
CONV_WIDTH = 512
CONV_K = 31
IN_COLS = 4 * RET_WIDTH + 2 * CONV_WIDTH
ROPE_BASE = 10000.0
N_EXPERTS = 32
TOP_K = 4
SWIGLU_LIMIT = 7.0
SWIGLU_ALPHA = 1.702
EPS = 1e-6
PAST_LEN = 2048

LANES = 128
SUBLANES = 8
SUB = 256
SUBTILES = 2
TILE = SUB * SUBTILES
COMBINE_TILE = 256
CONV_PAD = 32
CONV_ROWS = 32
MOE_TILE = 256
DMA_UNROLL = 8
FFN_CHUNKS = 4
GATHER_AHEAD = 2
X_BUFFERS = GATHER_AHEAD + 1
VMEM_LIMIT = 52 * 1024 * 1024
MIXER_VMEM_LIMIT = 60 * 1024 * 1024

F32 = jnp.float32
BF16 = jnp.bfloat16


def _dot(a, b):
    return jnp.dot(a, b, preferred_element_type=F32)


def _dot_nt(a, b):
    return lax.dot_general(a, b, (((1,), (1,)), ((), ())), preferred_element_type=F32)


def _dot_tn(a, b):
    return lax.dot_general(a, b, (((0,), (0,)), ((), ())), preferred_element_type=F32)


def _rms(x, g):
    return x * lax.rsqrt(jnp.mean(x * x, axis=-1, keepdims=True) + EPS) * g


def _sigmoid(x):
    return 1.0 / (1.0 + jnp.exp(-x))


TOKEN_ROWS = D_MODEL // LANES
assert TOKEN_ROWS == SUBLANES


def _store_token_major(ref, val):
    t = val.shape[0]
    for c in range(TOKEN_ROWS):
        ref[pl.ds(c, t, stride=TOKEN_ROWS), :] = val[:, c * LANES:(c + 1) * LANES]


def _load_token_major(ref, t):
    return jnp.concatenate([ref[pl.ds(c, t, stride=TOKEN_ROWS), :] for c in range(TOKEN_ROWS)], axis=-1)


def _mixer_body(nseg, seg, first, x_ref, cosf, sinf, mask_ref, qdec_ref, kdec_ref, cdec_ref,
                s_in, c_in, gmix_ref, win_ref, convw_ref, convb_ref, lng_ref, lnb_ref, retg_ref,
                wout_ref, gffn_ref, wr_ref, br_ref,
                h_ref, hn_ref, idx_ref, wts_ref, cnt_ref, s_out, c_out, ubuf, s_scr,
                seg0=0, ubuf_prev=None):
    carry = first is not None
    T = nseg * seg
    x = x_ref[...]
    xn = _rms(x, gmix_ref[...]).astype(BF16)
    proj = _dot(xn, win_ref[...])

    if carry:
        def continue_sequence():
            ubuf[0:CONV_PAD, :] = ubuf_prev[seg:seg + CONV_PAD, :]

        if first is False:
            continue_sequence()
        else:
            @pl.when(first)
            def _():
                s_scr[...] = jnp.zeros_like(s_scr)
                ubuf[0:CONV_PAD, :] = jnp.zeros((CONV_PAD, CONV_WIDTH), F32)

            pl.when(jnp.logical_not(first))(continue_sequence)

    o_heads = []
    for h in range(RET_HEADS):
        q = proj[:, h * RET_DK:(h + 1) * RET_DK]
        k = proj[:, RET_WIDTH + h * RET_DK:RET_WIDTH + (h + 1) * RET_DK]
        v = proj[:, 2 * RET_WIDTH + h * RET_DK:2 * RET_WIDTH + (h + 1) * RET_DK].astype(BF16)
        q = q * cosf + pltpu.roll(q, RET_DK // 2, 1) * sinf
        k = (k * cosf + pltpu.roll(k, RET_DK // 2, 1) * sinf) * (RET_DK ** -0.5)
        p = (_dot_nt(q.astype(BF16), k.astype(BF16)) * mask_ref[h]).astype(BF16)
        o = _dot(p, v)
        qd = (q * qdec_ref[h]).astype(BF16)
        kd = (k * kdec_ref[h]).astype(BF16)
        cdec = cdec_ref[h, 0:1, :]
        o_cross = []
        for s in range(nseg):
            rows = slice(s * seg, (s + 1) * seg)
            s_prev = s_scr[h] if carry else s_in[0, seg0 + s, h]
            o_cross.append(_dot(qd[rows], s_prev.astype(BF16)))
            s_new = cdec * s_prev + _dot_tn(kd[rows], v[rows])
            if carry:
                s_scr[h] = s_new
                s_out[0, 0, h] = s_new
            else:
                s_out[0, seg0 + s, h] = s_new
        o = o + (o_cross[0] if nseg == 1 else jnp.concatenate(o_cross, axis=0))
        o = o * lax.rsqrt(jnp.mean(o * o, axis=-1, keepdims=True) + EPS)
        g = proj[:, 3 * RET_WIDTH + h * RET_DK:3 * RET_WIDTH + (h + 1) * RET_DK]
        o_heads.append(o * retg_ref[:, h * RET_DK:(h + 1) * RET_DK] * (g * _sigmoid(g)))

    a = proj[:, 4 * RET_WIDTH:4 * RET_WIDTH + CONV_WIDTH]
    ga = proj[:, 4 * RET_WIDTH + CONV_WIDTH:]
    u = a * _sigmoid(ga)
    hist = CONV_PAD - (CONV_K - 1)
    cv_chunks = []
    for s in range(nseg):
        if not carry:
            ubuf[hist:CONV_PAD, :] = c_in[0, seg0 + s]
        ubuf[CONV_PAD:CONV_PAD + seg, :] = u[s * seg:(s + 1) * seg]
        for c in range(seg // CONV_ROWS):
            acc = jnp.broadcast_to(convb_ref[...], (CONV_ROWS, CONV_WIDTH))
            for shift in range(SUBLANES):
                rows = CONV_ROWS + (SUBLANES if shift else 0)
                part = None
                for j in range(CONV_K):
                    if (hist + j) % SUBLANES == shift:
                        r0 = c * CONV_ROWS + (hist + j) - shift
                        term = convw_ref[j:j + 1, :] * ubuf[r0:r0 + rows, :]
                        part = term if part is None else part + term
                acc = acc + part[shift:shift + CONV_ROWS, :]
            cv_chunks.append(acc)
        tail = ubuf[hist + seg:CONV_PAD + seg, :]
        if carry:
            c_out[0, 0] = tail
        else:
            c_out[0, seg0 + s] = tail
    cv = jnp.concatenate(cv_chunks, axis=0)
    mu = jnp.mean(cv, axis=-1, keepdims=True)
    xc = cv - mu
    ln = xc * lax.rsqrt(jnp.mean(xc * xc, axis=-1, keepdims=True) + EPS) * lng_ref[...] + lnb_ref[...]
    o_conv = ln * _sigmoid(ln)

    mixed = jnp.concatenate(o_heads + [o_conv], axis=-1).astype(BF16)
    hres = x + _dot(mixed, wout_ref[...])
    h_ref[...] = hres

    hn = _rms(hres, gffn_ref[...])
    _store_token_major(hn_ref, hn)
    hn_hi = hn.astype(BF16)
    hn_lo = (hn - hn_hi.astype(F32)).astype(BF16)
    r_hi = _dot_nt(wr_ref[...], hn_hi)
    r_lo = _dot_nt(wr_ref[...], hn_lo)
    logits = (r_hi[:N_EXPERTS] + r_hi[N_EXPERTS:2 * N_EXPERTS] + r_lo[:N_EXPERTS] + br_ref[...])
    expert = lax.broadcasted_iota(jnp.int32, (N_EXPERTS, T), 0).astype(F32)
    work = logits
    top_v, top_i = [], []
    picked = jnp.zeros((N_EXPERTS, T), F32)
    for kk in range(TOP_K):
        m = jnp.max(work, axis=0, keepdims=True)
        sel = jnp.min(jnp.where(work == m, expert, float(N_EXPERTS)), axis=0, keepdims=True)
        top_v.append(m)
        top_i.append(sel)
        hit = expert == sel
        picked = picked + hit.astype(F32)
        work = jnp.where(hit, -jnp.inf, work)
    ex = jnp.exp(jnp.concatenate(top_v, axis=0) - top_v[0])
    wts_ref[...] = ex / jnp.sum(ex, axis=0, keepdims=True)
    idx_ref[...] = jnp.concatenate(top_i, axis=0).astype(jnp.int32)
    cnt_ref[...] += jnp.broadcast_to(jnp.sum(picked, axis=1, keepdims=True), (N_EXPERTS, LANES))


def _mixer_kernel(n_prompt_tiles, tiles_per_seq, nseg, seg,
                  xp_ref, xs_ref, cb_ref, sb_ref, cr_ref, sr_ref, crs_ref, srs_ref, coss_ref, sins_ref,
                  maskp_ref, qdecp_ref, kdecp_ref, cdecp_ref,
                  masks_ref, qdecs_ref, kdecs_ref, cdecs_ref, s_in, c_in,
                  gmix_ref, win_ref, convw_ref, convb_ref, lng_ref, lnb_ref, retg_ref,
                  wout_ref, gffn_ref, wr_ref, br_ref,
                  h_ref, hn_ref, idx_ref, wts_ref, cnt_ref, sp_out, cp_out, ss_out, cs_out,
                  ubufs, s_scr):
    i = pl.program_id(0)
    weights = (gmix_ref, win_ref, convw_ref, convb_ref, lng_ref, lnb_ref, retg_ref,
               wout_ref, gffn_ref, wr_ref, br_ref)

    def rows(ref, k, per_row=1):
        return ref.at[pl.ds(k * SUB * per_row, SUB * per_row)]

    def outs(k):
        cols = pl.ds(k * SUB, SUB)
        return (rows(h_ref, k), rows(hn_ref, k, TOKEN_ROWS), idx_ref.at[:, cols], wts_ref.at[:, cols],
                cnt_ref)

    @pl.when(i == 0)
    def _():
        cnt_ref[...] = jnp.zeros_like(cnt_ref)

    @pl.when(i < n_prompt_tiles)
    def _():
        for k in range(SUBTILES):
            first = (i % tiles_per_seq == 0) if k == 0 else False
            base = pl.ds((i % tiles_per_seq) * SUBTILES + k, 1)
            cos_b, sin_b = cb_ref[base, :], sb_ref[base, :]
            cosf = cos_b * cr_ref[...] - sin_b * sr_ref[...]
            sinf = sin_b * crs_ref[...] + cos_b * srs_ref[...]
            _mixer_body(1, SUB, first, rows(xp_ref, k), cosf, sinf,
                        maskp_ref, qdecp_ref, kdecp_ref, cdecp_ref, None, None, *weights, *outs(k),
                        sp_out, cp_out, ubufs.at[k], s_scr, ubuf_prev=ubufs.at[(k - 1) % SUBTILES])

    @pl.when(i >= n_prompt_tiles)
    def _():
        for k in range(SUBTILES):
            _mixer_body(nseg, seg, None, rows(xs_ref, k), coss_ref[...], sins_ref[...],
                        masks_ref, qdecs_ref, kdecs_ref, cdecs_ref, s_in, c_in, *weights, *outs(k),
                        ss_out, cs_out, ubufs.at[k], None, seg0=k * nseg)


def _retention_tables(seg, nseg, lc):
    T = seg * nseg
    lg = jnp.log(1.0 - 2.0 ** (-5.0 - jnp.arange(RET_HEADS, dtype=F32)))
    i = jnp.arange(T, dtype=jnp.int32)
    loc = (i % seg).astype(F32)
    same_seg = (i[:, None] // seg) == (i[None, :] // seg)
    causal = ((i[None, :] % seg) // lc) <= ((i[:, None] % seg) // lc)
    dist = jnp.abs(loc[:, None] - loc[None, :])
    mask = jnp.where((same_seg & causal)[None], jnp.exp(lg[:, None, None] * dist[None]), 0.0)
    qdec = jnp.exp(lg[:, None] * (loc[None, :] + 1.0))
    kdec = jnp.exp(lg[:, None] * (seg - 1.0 - loc[None, :]))
    cdec = jnp.exp(lg * seg)
    bc = lambda t: jnp.broadcast_to(t[:, :, None], (RET_HEADS, t.shape[1], RET_DK))
    return (mask.astype(F32), bc(qdec), bc(kdec),
            jnp.broadcast_to(cdec[:, None, None], (RET_HEADS, 8, RET_DK)))


def _rope_cos_sin(pos):
    half = RET_DK // 2
    inv = ROPE_BASE ** (-jnp.arange(half, dtype=F32) / half)
    ang = pos[:, None] * inv[None, :]
    cos, sin = jnp.cos(ang), jnp.sin(ang)
    return jnp.concatenate([cos, cos], axis=-1), jnp.concatenate([sin, sin], axis=-1)


def _rope_sign():
    half = RET_DK // 2
    return jnp.concatenate([-jnp.ones((half,), F32), jnp.ones((half,), F32)])[None, :]


def _const_spec(shape):
    return pl.BlockSpec(shape, lambda *_: (0,) * len(shape), pipeline_mode=pl.Buffered(1))


def _weight_specs():
    return [
        _const_spec((1, D_MODEL)),
        _const_spec((D_MODEL, IN_COLS)),
        _const_spec((CONV_K, CONV_WIDTH)),
        _const_spec((1, CONV_WIDTH)),
        _const_spec((1, CONV_WIDTH)),
        _const_spec((1, CONV_WIDTH)),
        _const_spec((1, RET_WIDTH)),
        _const_spec((D_MODEL, D_MODEL)),
        _const_spec((1, D_MODEL)),
        _const_spec((LANES, D_MODEL)),
        _const_spec((N_EXPERTS, SUB)),
    ]


def _moe_kernel(te_ref, cnt_ref, nv_ref, first_ref, ahead_ref, dst_ref, hn_any, wg_ref, wu_ref, wd_ref,
                bg_ref, bu_ref, bd_ref, y_any, xbuf, ybuf, wg_bf, wu_bf, wd_bf, gsem, ssem):
    j = pl.program_id(0)
    nt = pl.num_programs(0)
    nvalid = nv_ref[0]
    slot = j % 2
    xslot = j % X_BUFFERS

    def token_rows(base, n=1):
        if not isinstance(base, int):
            base = pl.multiple_of(base, TOKEN_ROWS)
        return pl.ds(base, n * TOKEN_ROWS)

    def gather_copy(idx_ref, r, sl, t=0):
        return pltpu.make_async_copy(hn_any.at[token_rows(idx_ref[t, 0, r])],
                                     xbuf.at[sl, token_rows(r * TOKEN_ROWS)], gsem.at[sl])

    def scatter_copy(r, sl):
        return pltpu.make_async_copy(ybuf.at[sl, token_rows(r * TOKEN_ROWS)],
                                     y_any.at[token_rows(dst_ref[0, 0, r])], ssem.at[sl])

    def start_gather(idx_ref, sl, t=0):
        def body(r, c):
            gather_copy(idx_ref, r, sl, t).start()
            return c
        lax.fori_loop(0, MOE_TILE, body, 0, unroll=DMA_UNROLL)

    def wait_gather(sl):
        pltpu.make_async_copy(hn_any.at[token_rows(0, MOE_TILE)], xbuf.at[sl], gsem.at[sl]).wait()

    def start_scatter(tile):
        n = cnt_ref[tile]
        groups = n // DMA_UNROLL

        def group(g, c):
            for u in range(DMA_UNROLL):
                scatter_copy(g * DMA_UNROLL + u, tile % 2).start()
            return c
        lax.fori_loop(0, groups, group, 0)

        def single(r, c):
            scatter_copy(r, tile % 2).start()
            return c
        lax.fori_loop(groups * DMA_UNROLL, n, single, 0)

    def wait_scatter(tile):
        n = cnt_ref[tile]
        pltpu.make_async_copy(ybuf.at[tile % 2, token_rows(0, n)], y_any.at[token_rows(0, n)],
                              ssem.at[tile % 2]).wait()

    @pl.when(j == 0)
    def _():
        for t in range(GATHER_AHEAD):
            @pl.when(t < nvalid)
            def _():
                start_gather(first_ref, t, t)

    ahead_slot = (j + GATHER_AHEAD) % X_BUFFERS

    def compute(before_store, dma_starts=()):
        starts = list(dma_starts)
        per_chunk = -(-len(starts) // (3 * FFN_CHUNKS))
        width = D_MODEL // FFN_CHUNKS

        def issue_some():
            for _ in range(min(per_chunk, len(starts))):
                starts.pop(0)()

        x = _load_token_major(xbuf.at[xslot], MOE_TILE).astype(BF16)
        gate, act, out = [], [], []
        for c in range(FFN_CHUNKS):
            cols = slice(c * width, (c + 1) * width)
            gate.append(jnp.minimum(_dot(x, wg_bf[:, cols]) + bg_ref[0, :, cols], SWIGLU_LIMIT))
            issue_some()
        for c in range(FFN_CHUNKS):
            cols = slice(c * width, (c + 1) * width)
            up = jnp.clip(_dot(x, wu_bf[:, cols]) + bu_ref[0, :, cols], -SWIGLU_LIMIT, SWIGLU_LIMIT)
            act.append((gate[c] * _sigmoid(SWIGLU_ALPHA * gate[c]) * (up + 1.0)).astype(BF16))
            issue_some()
        act = jnp.concatenate(act, axis=-1)
        for c in range(FFN_CHUNKS):
            cols = slice(c * width, (c + 1) * width)
            out.append(_dot(act, wd_bf[:, cols]) + bd_ref[0, :, cols])
            issue_some()
        assert not starts
        before_store()
        _store_token_major(ybuf.at[slot], jnp.concatenate(out, axis=-1))

    @pl.when(j < nvalid)
    def _():
        e_prev = te_ref[jnp.maximum(j - 1, 0)]
        changed = jnp.logical_or(j == 0, e_prev != te_ref[j])

        @pl.when(changed)
        def _():
            wg_bf[...] = wg_ref[0].astype(BF16)
            wu_bf[...] = wu_ref[0].astype(BF16)
            wd_bf[...] = wd_ref[0].astype(BF16)

        wait_gather(xslot)
        steady = jnp.logical_and(jnp.logical_and(j >= 2, j + GATHER_AHEAD < nvalid),
                                 cnt_ref[jnp.maximum(j - 1, 0)] == MOE_TILE)

        @pl.when(steady)
        def _():
            starts = []
            for r in range(MOE_TILE):
                starts.append(gather_copy(ahead_ref, r, ahead_slot).start)
                starts.append(functools.partial(scatter_copy(r, 1 - slot).start, priority=1))
            compute(lambda: wait_scatter(j - 2), starts)

        @pl.when(jnp.logical_not(steady))
        def _():
            @pl.when(j + GATHER_AHEAD < nvalid)
            def _():
                start_gather(ahead_ref, ahead_slot)

            @pl.when(j >= 1)
            def _():
                start_scatter(j - 1)

            def retire():
                @pl.when(j >= 2)
                def _():
                    wait_scatter(j - 2)
            compute(retire)

    @pl.when(j == nvalid)
    def _():
        start_scatter(j - 1)

    @pl.when(j == nt - 1)
    def _():
        wait_scatter(nvalid - 1)

        @pl.when(nvalid >= 2)
        def _():
            wait_scatter(nvalid - 2)


def _combine_kernel(h_ref, y0_ref, y1_ref, y2_ref, y3_ref, w_ref, g_ref, o_ref):
    w = w_ref[...]
    acc = h_ref[...]
    for kk, y_ref in enumerate((y0_ref, y1_ref, y2_ref, y3_ref)):
        acc = acc + w[:, kk:kk + 1] * _load_token_major(y_ref, COMBINE_TILE)
    o_ref[...] = _rms(acc, g_ref[...])


def kernel(x_prompt, x_sample, state_ret, cache_conv, norm_mix_g, w_in, conv_w, conv_b, conv_ln_g,
           conv_ln_b, ret_norm_g, w_out, norm_ffn_g, w_router, b_router, w_e_gate, b_e_gate, w_e_up,
           b_e_up, w_e_down, b_e_down, norm_final_g):
    bp, lp, _ = x_prompt.shape
    bs, ls, _ = x_sample.shape
    depth = norm_mix_g.shape[0]
    assert depth == 1 and lp % TILE == 0 and TILE % ls == 0 and (bs * ls) % TILE == 0
    np_tok, ns_tok = bp * lp, bs * ls
    n_tok = np_tok + ns_tok
    nseg = SUB // ls

    row = lambda t: t.reshape(1, -1)
    wr_hi = w_router[0].astype(BF16)
    wr_lo = (w_router[0] - wr_hi.astype(F32)).astype(BF16)
    wr_cat = jnp.concatenate(
        [wr_hi, wr_lo, jnp.zeros((D_MODEL, LANES - 2 * N_EXPERTS), BF16)], axis=1).T
    br_bcast = jnp.broadcast_to(b_router[0][:, None], (N_EXPERTS, SUB))
    weights = (row(norm_mix_g[0]), w_in[0].astype(BF16), conv_w[0], row(conv_b[0]), row(conv_ln_g[0]),
               row(conv_ln_b[0]), row(ret_norm_g[0]), w_out[0].astype(BF16), row(norm_ffn_g[0]),
               wr_cat, br_bcast)

    sign = _rope_sign()
    cos_b, sin_b = _rope_cos_sin(jnp.arange(lp // SUB, dtype=F32) * SUB)
    cos_r, sin_r = _rope_cos_sin(jnp.arange(SUB, dtype=F32))
    rope_p = (cos_b, sin_b, cos_r, sin_r, cos_r * sign, sin_r * sign)
    cos_s, sin_s = _rope_cos_sin(jnp.arange(ls, dtype=F32) + PAST_LEN)
    cos_s, sin_s = jnp.tile(cos_s, (nseg, 1)), jnp.tile(sin_s * sign, (nseg, 1))
    tabs_p = _retention_tables(SUB, 1, CHUNK)
    tabs_s = _retention_tables(ls, nseg, min(CHUNK, ls))
    tiles_seq = lp // TILE
    n_ptiles = np_tok // TILE
    n_stiles = ns_tok // TILE
    p_tile = lambda i: jnp.minimum(i, n_ptiles - 1)
    s_tile = lambda i: jnp.maximum(i - n_ptiles, 0)
    tok_spec = lambda w: pl.BlockSpec((TILE, w), lambda i: (i, 0))
    tab_specs = [_const_spec((RET_HEADS, SUB, SUB)), _const_spec((RET_HEADS, SUB, RET_DK)),
                 _const_spec((RET_HEADS, SUB, RET_DK)), _const_spec((RET_HEADS, 8, RET_DK))]
    tile_seqs = SUBTILES * nseg
    state_s_spec = pl.BlockSpec((1, tile_seqs, RET_HEADS, RET_DK, RET_DK),
                                lambda i: (0, s_tile(i), 0, 0, 0))
    cache_s_spec = pl.BlockSpec((1, tile_seqs, CONV_K - 1, CONV_WIDTH), lambda i: (0, s_tile(i), 0, 0))
    h_all, hn_all, idx_t, wts_t, cnt_all, state_p, cache_p, state_s, cache_s = pl.pallas_call(
        functools.partial(_mixer_kernel, n_ptiles, tiles_seq, nseg, ls),
        grid=(n_ptiles + n_stiles,),
        in_specs=[pl.BlockSpec((TILE, D_MODEL), lambda i: (p_tile(i), 0)),
                  pl.BlockSpec((TILE, D_MODEL), lambda i: (s_tile(i), 0)),
                  _const_spec((lp // SUB, RET_DK)), _const_spec((lp // SUB, RET_DK))]
                 + [_const_spec((SUB, RET_DK))] * 4
                 + [_const_spec((SUB, RET_DK)), _const_spec((SUB, RET_DK))]
                 + tab_specs + tab_specs + [state_s_spec, cache_s_spec] + _weight_specs(),
        out_specs=[tok_spec(D_MODEL), pl.BlockSpec((TILE * TOKEN_ROWS, LANES), lambda i: (i, 0)),
                   pl.BlockSpec((TOP_K, TILE), lambda i: (0, i)),
                   pl.BlockSpec((TOP_K, TILE), lambda i: (0, i)),
                   pl.BlockSpec((N_EXPERTS, LANES), lambda i: (0, 0)),
                   pl.BlockSpec((1, 1, RET_HEADS, RET_DK, RET_DK),
                                lambda i: (0, p_tile(i) // tiles_seq, 0, 0, 0)),
                   pl.BlockSpec((1, 1, CONV_K - 1, CONV_WIDTH),
                                lambda i: (0, p_tile(i) // tiles_seq, 0, 0)),
                   state_s_spec, cache_s_spec],
        out_shape=[jax.ShapeDtypeStruct((n_tok, D_MODEL), F32),
                   jax.ShapeDtypeStruct((n_tok * TOKEN_ROWS, LANES), F32),
                   jax.ShapeDtypeStruct((TOP_K, n_tok), jnp.int32),
                   jax.ShapeDtypeStruct((TOP_K, n_tok), F32),
                   jax.ShapeDtypeStruct((N_EXPERTS, LANES), F32),
                   jax.ShapeDtypeStruct((1, bp, RET_HEADS, RET_DK, RET_DK), F32),
                   jax.ShapeDtypeStruct((1, bp, CONV_K - 1, CONV_WIDTH), F32),
                   jax.ShapeDtypeStruct(state_ret.shape, F32),
                   jax.ShapeDtypeStruct(cache_conv.shape, F32)],
        scratch_shapes=[pltpu.VMEM((SUBTILES, CONV_PAD + SUB, CONV_WIDTH), F32),
                        pltpu.VMEM((RET_HEADS, RET_DK, RET_DK), F32)],
        compiler_params=pltpu.CompilerParams(dimension_semantics=("arbitrary",),
                                             vmem_limit_bytes=MIXER_VMEM_LIMIT),
        name="mixer",
    )(x_prompt.reshape(np_tok, D_MODEL), x_sample.reshape(ns_tok, D_MODEL), *rope_p, cos_s, sin_s,
      *tabs_p, *tabs_s, state_ret, cache_conv, *weights)

    n_pair = n_tok * TOP_K
    n_tiles = n_pair // MOE_TILE + N_EXPERTS
    n_slot = n_tiles * MOE_TILE
    e_flat = idx_t.reshape(n_pair)
    counts = cnt_all[:, 0].astype(jnp.int32)
    tiles_e = (counts + MOE_TILE - 1) // MOE_TILE
    tile_end = jnp.cumsum(tiles_e)
    n_valid = tile_end[-1]
    tile_start = tile_end - tiles_e
    pair_bits = (n_pair - 1).bit_length()
    pad_mark = (1 << pair_bits) - 1
    assert pad_mark >= n_pair and (N_EXPERTS + 1) << pair_bits < 2 ** 31
    pad_end = jnp.cumsum(tiles_e * MOE_TILE - counts)
    pad_id = jnp.arange(n_slot - n_pair, dtype=jnp.int32)
    pad_e = jnp.sum((pad_end[None, :] <= pad_id[:, None]).astype(jnp.int32), axis=1)
    pair = jnp.arange(n_pair, dtype=jnp.int32)
    keys = jnp.concatenate([(e_flat << pair_bits) | pair, (pad_e << pair_bits) | pad_mark])
    slot_pair = lax.sort(keys, is_stable=False) & pad_mark
    slot_pair = jnp.where(slot_pair == pad_mark, 0, slot_pair)
    slot_src = (slot_pair % n_tok) * TOKEN_ROWS
    slot_dst = slot_pair * TOKEN_ROWS
    tile_id = jnp.arange(n_tiles, dtype=jnp.int32)
    tile_q = jnp.minimum(tile_id, n_valid - 1)
    tile_e = jnp.sum((tile_end[None, :] <= tile_q[:, None]).astype(jnp.int32), axis=1)
    tile_e = jnp.minimum(tile_e, N_EXPERTS - 1).astype(jnp.int32)
    of_tile = (tile_e[:, None] == jnp.arange(N_EXPERTS, dtype=jnp.int32)[None, :]).astype(jnp.int32)
    rows_left = jnp.sum(of_tile * (counts + tile_start * MOE_TILE)[None, :], axis=1) - tile_id * MOE_TILE
    tile_cnt = jnp.where(tile_id < n_valid, jnp.clip(rows_left, 0, MOE_TILE), 0).astype(jnp.int32)
    slot_src = slot_src.reshape(n_tiles, 1, MOE_TILE)
    slot_dst = slot_dst.reshape(n_tiles, 1, MOE_TILE)

    any_spec = pl.BlockSpec(memory_space=pl.ANY)
    smem_spec = lambda fn: pl.BlockSpec((1, 1, MOE_TILE), fn, memory_space=pltpu.SMEM)
    w_spec = pl.BlockSpec((1, D_MODEL, D_MODEL), lambda j, te, cnt, nv: (te[j], 0, 0))
    b_spec = pl.BlockSpec((1, 1, D_MODEL), lambda j, te, cnt, nv: (te[j], 0, 0))
    y_rows = pl.pallas_call(
        _moe_kernel,
        grid_spec=pltpu.PrefetchScalarGridSpec(
            num_scalar_prefetch=3,
            grid=(n_tiles,),
            in_specs=[pl.BlockSpec((GATHER_AHEAD, 1, MOE_TILE), lambda j, te, cnt, nv: (0, 0, 0),
                                   memory_space=pltpu.SMEM),
                      smem_spec(lambda j, te, cnt, nv: (jnp.minimum(j + GATHER_AHEAD, n_tiles - 1), 0, 0)),
                      smem_spec(lambda j, te, cnt, nv: (jnp.maximum(j - 1, 0), 0, 0)),
                      any_spec, w_spec, w_spec, w_spec, b_spec, b_spec, b_spec],
            out_specs=any_spec,
            scratch_shapes=[pltpu.VMEM((X_BUFFERS, MOE_TILE * TOKEN_ROWS, LANES), F32),
                            pltpu.VMEM((2, MOE_TILE * TOKEN_ROWS, LANES), F32),
                            pltpu.VMEM((D_MODEL, D_MODEL), BF16),
                            pltpu.VMEM((D_MODEL, D_MODEL), BF16),
                            pltpu.VMEM((D_MODEL, D_MODEL), BF16),
                            pltpu.SemaphoreType.DMA((X_BUFFERS,)),
                            pltpu.SemaphoreType.DMA((2,))]),
        out_shape=jax.ShapeDtypeStruct((n_pair * TOKEN_ROWS, LANES), F32),
        compiler_params=pltpu.CompilerParams(dimension_semantics=("arbitrary",),
                                             vmem_limit_bytes=VMEM_LIMIT),
        name="moe_experts",
    )(tile_e, tile_cnt, n_valid.reshape(1).astype(jnp.int32), slot_src, slot_src, slot_dst, hn_all,
      w_e_gate[0], w_e_up[0], w_e_down[0],
      b_e_gate[0].reshape(N_EXPERTS, 1, D_MODEL), b_e_up[0].reshape(N_EXPERTS, 1, D_MODEL),
      b_e_down[0].reshape(N_EXPERTS, 1, D_MODEL))

    def combine(row0, rows):
        ct = COMBINE_TILE
        blk0 = row0 // ct
        planes = [pl.BlockSpec((ct * TOKEN_ROWS, LANES), functools.partial(
            lambda i, kk: (kk * (n_tok // ct) + blk0 + i, 0), kk=kk)) for kk in range(TOP_K)]
        return pl.pallas_call(
            _combine_kernel,
            grid=(rows // ct,),
            in_specs=[pl.BlockSpec((ct, D_MODEL), lambda i: (blk0 + i, 0))] + planes
                     + [pl.BlockSpec((ct, TOP_K), lambda i: (blk0 + i, 0)), _const_spec((1, D_MODEL))],
            out_specs=pl.BlockSpec((ct, D_MODEL), lambda i: (i, 0)),
            out_shape=jax.ShapeDtypeStruct((rows, D_MODEL), F32),
            compiler_params=pltpu.CompilerParams(dimension_semantics=("arbitrary",),
                                                 vmem_limit_bytes=VMEM_LIMIT),
            name="combine",
        )(h_all, y_rows, y_rows, y_rows, y_rows, wts_all, row(norm_final_g))

    wts_all = wts_t.T

    y_prompt = combine(0, np_tok).reshape(bp, lp, D_MODEL)
    y_sample = combine(np_tok, ns_tok).reshape(bs, ls, D_MODEL)
    return (y_prompt, y_sample, state_p, cache_p, state_s, cache_s)
```

```python
import functools

import jax
import jax.numpy as jnp
from jax import lax
from jax.experimental import pallas as pl
from jax.experimental.pallas import tpu as pltpu

D_MODEL = 1024
CHUNK = 64
RET_WIDTH = 512
RET_HEADS = 4
RET_DK = 128
CONV_WIDTH = 512
CONV_K = 31
IN_COLS = 4 * RET_WIDTH + 2 * CONV_WIDTH
ROPE_BASE = 10000.0
N_EXPERTS = 32
TOP_K = 4
SWIGLU_LIMIT = 7.0
SWIGLU_ALPHA = 1.702
EPS = 1e-6
PAST_LEN = 2048

LANES = 128
SUBLANES = 8
SUB = 256
SUBTILES = 2
TILE = SUB * SUBTILES
COMBINE_TILE = 256
CONV_PAD = 32
CONV_ROWS = 32
MOE_TILE = 256
DMA_UNROLL = 8
FFN_CHUNKS = 4
GATHER_AHEAD = 2
X_BUFFERS = GATHER_AHEAD + 1
VMEM_LIMIT = 52 * 1024 * 1024
MIXER_VMEM_LIMIT = 62 * 1024 * 1024

F32 = jnp.float32
BF16 = jnp.bfloat16


def _dot(a, b):
    return jnp.dot(a, b, preferred_element_type=F32)


def _dot_nt(a, b):
    return lax.dot_general(a, b, (((1,), (1,)), ((), ())), preferred_element_type=F32)


def _dot_tn(a, b):
    return lax.dot_general(a, b, (((0,), (0,)), ((), ())), preferred_element_type=F32)


def _rms(x, g):
    return x * lax.rsqrt(jnp.mean(x * x, axis=-1, keepdims=True) + EPS) * g


def _sigmoid(x):
    return 1.0 / (1.0 + jnp.exp(-x))


TOKEN_ROWS = D_MODEL // LANES
assert TOKEN_ROWS == SUBLANES


def _store_token_major(ref, val):
    t = val.shape[0]
    for c in range(TOKEN_ROWS):
        ref[pl.ds(c, t, stride=TOKEN_ROWS), :] = val[:, c * LANES:(c + 1) * LANES]


def _load_token_major(ref, t):
    return jnp.concatenate([ref[pl.ds(c, t, stride=TOKEN_ROWS), :] for c in range(TOKEN_ROWS)], axis=-1)


def _project(x_ref, gmix_ref, win_ref):
    return _dot(_rms(x_ref[...], gmix_ref[...]).astype(BF16), win_ref[...])


def _mixer_body(nseg, seg, first, x_ref, cosf, sinf, mask_ref, qdec_ref, kdec_ref, cdec_ref,
                s_in, c_in, gmix_ref, win_ref, convw_ref, convb_ref, lng_ref, lnb_ref, retg_ref,
                wout_ref, gffn_ref, wr_ref, br_ref,
                h_ref, hn_ref, idx_ref, wts_ref, cnt_ref, s_out, c_out, ubuf, s_scr,
                seg0=0, ubuf_prev=None, proj=None):
    carry = first is not None
    T = nseg * seg
    x = x_ref[...]

    def fresh(val):
        return val if first is False else jnp.where(first, 0.0, val)

    if carry:
        ubuf[0:CONV_PAD, :] = fresh(ubuf_prev[seg:seg + CONV_PAD, :])

    o_heads = []
    for h in range(RET_HEADS):
        q = proj[:, h * RET_DK:(h + 1) * RET_DK]
        k = proj[:, RET_WIDTH + h * RET_DK:RET_WIDTH + (h + 1) * RET_DK]
        v = proj[:, 2 * RET_WIDTH + h * RET_DK:2 * RET_WIDTH + (h + 1) * RET_DK].astype(BF16)
        q = q * cosf + pltpu.roll(q, RET_DK // 2, 1) * sinf
        k = (k * cosf + pltpu.roll(k, RET_DK // 2, 1) * sinf) * (RET_DK ** -0.5)
        p = (_dot_nt(q.astype(BF16), k.astype(BF16)) * mask_ref[h]).astype(BF16)
        o = _dot(p, v)
        qd = (q * qdec_ref[h]).astype(BF16)
        kd = (k * kdec_ref[h]).astype(BF16)
        cdec = cdec_ref[h, 0:1, :]
        o_cross = []
        for s in range(nseg):
            rows = slice(s * seg, (s + 1) * seg)
            s_prev = fresh(s_scr[h]) if carry else s_in[0, seg0 + s, h]
            o_cross.append(_dot(qd[rows], s_prev.astype(BF16)))
            s_new = cdec * s_prev + _dot_tn(kd[rows], v[rows])
            if carry:
                s_scr[h] = s_new
                s_out[0, 0, h] = s_new
            else:
                s_out[0, seg0 + s, h] = s_new
        o = o + (o_cross[0] if nseg == 1 else jnp.concatenate(o_cross, axis=0))
        o = o * lax.rsqrt(jnp.mean(o * o, axis=-1, keepdims=True) + EPS)
        g = proj[:, 3 * RET_WIDTH + h * RET_DK:3 * RET_WIDTH + (h + 1) * RET_DK]
        o_heads.append(o * retg_ref[:, h * RET_DK:(h + 1) * RET_DK] * (g * _sigmoid(g)))

    a = proj[:, 4 * RET_WIDTH:4 * RET_WIDTH + CONV_WIDTH]
    ga = proj[:, 4 * RET_WIDTH + CONV_WIDTH:]
    u = a * _sigmoid(ga)
    hist = CONV_PAD - (CONV_K - 1)
    cv_chunks = []
    for s in range(nseg):
        if not carry:
            ubuf[hist:CONV_PAD, :] = c_in[0, seg0 + s]
        ubuf[CONV_PAD:CONV_PAD + seg, :] = u[s * seg:(s + 1) * seg]
        for c in range(seg // CONV_ROWS):
            acc = jnp.broadcast_to(convb_ref[...], (CONV_ROWS, CONV_WIDTH))
            for shift in range(SUBLANES):
                rows = CONV_ROWS + (SUBLANES if shift else 0)
                part = None
                for j in range(CONV_K):
                    if (hist + j) % SUBLANES == shift:
                        r0 = c * CONV_ROWS + (hist + j) - shift
                        term = convw_ref[j:j + 1, :] * ubuf[r0:r0 + rows, :]
                        part = term if part is None else part + term
                acc = acc + part[shift:shift + CONV_ROWS, :]
            cv_chunks.append(acc)
        tail = ubuf[hist + seg:CONV_PAD + seg, :]
        if carry:
            c_out[0, 0] = tail
        else:
            c_out[0, seg0 + s] = tail
    cv = jnp.concatenate(cv_chunks, axis=0)
    mu = jnp.mean(cv, axis=-1, keepdims=True)
    xc = cv - mu
    ln = xc * lax.rsqrt(jnp.mean(xc * xc, axis=-1, keepdims=True) + EPS) * lng_ref[...] + lnb_ref[...]
    o_conv = ln * _sigmoid(ln)

    mixed = jnp.concatenate(o_heads + [o_conv], axis=-1).astype(BF16)
    hres = x + _dot(mixed, wout_ref[...])
    h_ref[...] = hres

    hn = _rms(hres, gffn_ref[...])
    _store_token_major(hn_ref, hn)
    hn_hi = hn.astype(BF16)
    hn_lo = (hn - hn_hi.astype(F32)).astype(BF16)
    r_hi = _dot_nt(wr_ref[...], hn_hi)
    r_lo = _dot_nt(wr_ref[...], hn_lo)
    logits = (r_hi[:N_EXPERTS] + r_hi[N_EXPERTS:2 * N_EXPERTS] + r_lo[:N_EXPERTS] + br_ref[...])
    expert = lax.broadcasted_iota(jnp.int32, (N_EXPERTS, T), 0).astype(F32)
    work = logits
    top_v, top_i = [], []
    picked = jnp.zeros((N_EXPERTS, T), F32)
    for kk in range(TOP_K):
        m = jnp.max(work, axis=0, keepdims=True)
        sel = jnp.min(jnp.where(work == m, expert, float(N_EXPERTS)), axis=0, keepdims=True)
        top_v.append(m)
        top_i.append(sel)
        hit = expert == sel
        picked = picked + hit.astype(F32)
        work = jnp.where(hit, -jnp.inf, work)
    ex = jnp.exp(jnp.concatenate(top_v, axis=0) - top_v[0])
    wts_ref[...] = ex / jnp.sum(ex, axis=0, keepdims=True)
    idx_ref[...] = jnp.concatenate(top_i, axis=0).astype(jnp.int32)
    cnt_ref[...] += jnp.broadcast_to(jnp.sum(picked, axis=1, keepdims=True), (N_EXPERTS, LANES))


def _mixer_kernel(n_prompt_tiles, tiles_per_seq, nseg, seg,
                  xp_ref, xs_ref, cb_ref, sb_ref, cr_ref, sr_ref, crs_ref, srs_ref, coss_ref, sins_ref,
                  maskp_ref, qdecp_ref, kdecp_ref, cdecp_ref,
                  masks_ref, qdecs_ref, kdecs_ref, cdecs_ref, s_in, c_in,
                  gmix_ref, win_ref, convw_ref, convb_ref, lng_ref, lnb_ref, retg_ref,
                  wout_ref, gffn_ref, wr_ref, br_ref,
                  h_ref, hn_ref, idx_ref, wts_ref, cnt_ref, sp_out, cp_out, ss_out, cs_out,
                  ubufs, s_scr):
    i = pl.program_id(0)
    weights = (gmix_ref, win_ref, convw_ref, convb_ref, lng_ref, lnb_ref, retg_ref,
               wout_ref, gffn_ref, wr_ref, br_ref)

    def rows(ref, k, per_row=1):
        return ref.at[pl.ds(k * SUB * per_row, SUB * per_row)]

    def outs(k):
        cols = pl.ds(k * SUB, SUB)
        return (rows(h_ref, k), rows(hn_ref, k, TOKEN_ROWS), idx_ref.at[:, cols], wts_ref.at[:, cols],
                cnt_ref)

    @pl.when(i == 0)
    def _():
        cnt_ref[...] = jnp.zeros_like(cnt_ref)
        s_scr[...] = jnp.zeros_like(s_scr)
        ubufs[...] = jnp.zeros_like(ubufs)

    @pl.when(i < n_prompt_tiles)
    def _():
        projs = [_project(rows(xp_ref, k), gmix_ref, win_ref) for k in range(SUBTILES)]
        for k in range(SUBTILES):
            first = (i % tiles_per_seq == 0) if k == 0 else False
            base = pl.ds((i % tiles_per_seq) * SUBTILES + k, 1)
            cos_b, sin_b = cb_ref[base, :], sb_ref[base, :]
            cosf = cos_b * cr_ref[...] - sin_b * sr_ref[...]
            sinf = sin_b * crs_ref[...] + cos_b * srs_ref[...]
            _mixer_body(1, SUB, first, rows(xp_ref, k), cosf, sinf,
                        maskp_ref, qdecp_ref, kdecp_ref, cdecp_ref, None, None, *weights, *outs(k),
                        sp_out, cp_out, ubufs.at[k], s_scr, ubuf_prev=ubufs.at[(k - 1) % SUBTILES],
                        proj=projs[k])

    @pl.when(i >= n_prompt_tiles)
    def _():
        projs = [_project(rows(xs_ref, k), gmix_ref, win_ref) for k in range(SUBTILES)]
        for k in range(SUBTILES):
            _mixer_body(nseg, seg, None, rows(xs_ref, k), coss_ref[...], sins_ref[...],
                        masks_ref, qdecs_ref, kdecs_ref, cdecs_ref, s_in, c_in, *weights, *outs(k),
                        ss_out, cs_out, ubufs.at[k], None, seg0=k * nseg, proj=projs[k])


def _retention_tables(seg, nseg, lc):
    T = seg * nseg
    lg = jnp.log(1.0 - 2.0 ** (-5.0 - jnp.arange(RET_HEADS, dtype=F32)))
    i = jnp.arange(T, dtype=jnp.int32)
    loc = (i % seg).astype(F32)
    same_seg = (i[:, None] // seg) == (i[None, :] // seg)
    causal = ((i[None, :] % seg) // lc) <= ((i[:, None] % seg) // lc)
    dist = jnp.abs(loc[:, None] - loc[None, :])
    mask = jnp.where((same_seg & causal)[None], jnp.exp(lg[:, None, None] * dist[None]), 0.0)
    qdec = jnp.exp(lg[:, None] * (loc[None, :] + 1.0))
    kdec = jnp.exp(lg[:, None] * (seg - 1.0 - loc[None, :]))
    cdec = jnp.exp(lg * seg)
    bc = lambda t: jnp.broadcast_to(t[:, :, None], (RET_HEADS, t.shape[1], RET_DK))
    return (mask.astype(F32), bc(qdec), bc(kdec),
            jnp.broadcast_to(cdec[:, None, None], (RET_HEADS, 8, RET_DK)))


def _rope_cos_sin(pos):
    half = RET_DK // 2
    inv = ROPE_BASE ** (-jnp.arange(half, dtype=F32) / half)
    ang = pos[:, None] * inv[None, :]
    cos, sin = jnp.cos(ang), jnp.sin(ang)
    return jnp.concatenate([cos, cos], axis=-1), jnp.concatenate([sin, sin], axis=-1)


def _rope_sign():
    half = RET_DK // 2
    return jnp.concatenate([-jnp.ones((half,), F32), jnp.ones((half,), F32)])[None, :]


def _const_spec(shape):
    return pl.BlockSpec(shape, lambda *_: (0,) * len(shape), pipeline_mode=pl.Buffered(1))


def _weight_specs():
    return [
        _const_spec((1, D_MODEL)),
        _const_spec((D_MODEL, IN_COLS)),
        _const_spec((CONV_K, CONV_WIDTH)),
        _const_spec((1, CONV_WIDTH)),
        _const_spec((1, CONV_WIDTH)),
        _const_spec((1, CONV_WIDTH)),
        _const_spec((1, RET_WIDTH)),
        _const_spec((D_MODEL, D_MODEL)),
        _const_spec((1, D_MODEL)),
        _const_spec((LANES, D_MODEL)),
        _const_spec((N_EXPERTS, SUB)),
    ]


def _moe_kernel(te_ref, cnt_ref, nv_ref, first_ref, ahead_ref, dst_ref, hn_any, wg_ref, wu_ref, wd_ref,
                bg_ref, bu_ref, bd_ref, y_any, xbuf, ybuf, wg_bf, wu_bf, wd_bf, gsem, ssem):
    j = pl.program_id(0)
    nt = pl.num_programs(0)
    nvalid = nv_ref[0]
    slot = j % 2
    xslot = j % X_BUFFERS

    def token_rows(base, n=1):
        if not isinstance(base, int):
            base = pl.multiple_of(base, TOKEN_ROWS)
        return pl.ds(base, n * TOKEN_ROWS)

    def gather_copy(idx_ref, r, sl, t=0):
        return pltpu.make_async_copy(hn_any.at[token_rows(idx_ref[t, 0, r])],
                                     xbuf.at[sl, token_rows(r * TOKEN_ROWS)], gsem.at[sl])

    def scatter_copy(r, sl):
        return pltpu.make_async_copy(ybuf.at[sl, token_rows(r * TOKEN_ROWS)],
                                     y_any.at[token_rows(dst_ref[0, 0, r])], ssem.at[sl])

    def start_gather(idx_ref, sl, t=0):
        def body(r, c):
            gather_copy(idx_ref, r, sl, t).start()
            return c
        lax.fori_loop(0, MOE_TILE, body, 0, unroll=DMA_UNROLL)

    def wait_gather(sl):
        pltpu.make_async_copy(hn_any.at[token_rows(0, MOE_TILE)], xbuf.at[sl], gsem.at[sl]).wait()

    def start_scatter(tile):
        n = cnt_ref[tile]
        groups = n // DMA_UNROLL

        def group(g, c):
            for u in range(DMA_UNROLL):
                scatter_copy(g * DMA_UNROLL + u, tile % 2).start()
            return c
        lax.fori_loop(0, groups, group, 0)

        def single(r, c):
            scatter_copy(r, tile % 2).start()
            return c
        lax.fori_loop(groups * DMA_UNROLL, n, single, 0)

    def wait_scatter(tile):
        n = cnt_ref[tile]
        pltpu.make_async_copy(ybuf.at[tile % 2, token_rows(0, n)], y_any.at[token_rows(0, n)],
                              ssem.at[tile % 2]).wait()

    @pl.when(j == 0)
    def _():
        for t in range(GATHER_AHEAD):
            @pl.when(t < nvalid)
            def _():
                start_gather(first_ref, t, t)

    ahead_slot = (j + GATHER_AHEAD) % X_BUFFERS

    def compute(before_store, dma_starts=()):
        starts = list(dma_starts)
        per_chunk = -(-len(starts) // (3 * FFN_CHUNKS))
        width = D_MODEL // FFN_CHUNKS

        def issue_some():
            for _ in range(min(per_chunk, len(starts))):
                starts.pop(0)()

        x = _load_token_major(xbuf.at[xslot], MOE_TILE).astype(BF16)
        gate, act, out = [], [], []
        for c in range(FFN_CHUNKS):
            cols = slice(c * width, (c + 1) * width)
            gate.append(jnp.minimum(_dot(x, wg_bf[:, cols]) + bg_ref[0, :, cols], SWIGLU_LIMIT))
            issue_some()
        for c in range(FFN_CHUNKS):
            cols = slice(c * width, (c + 1) * width)
            up = jnp.clip(_dot(x, wu_bf[:, cols]) + bu_ref[0, :, cols], -SWIGLU_LIMIT, SWIGLU_LIMIT)
            act.append((gate[c] * _sigmoid(SWIGLU_ALPHA * gate[c]) * (up + 1.0)).astype(BF16))
            issue_some()
        act = jnp.concatenate(act, axis=-1)
        for c in range(FFN_CHUNKS):
            cols = slice(c * width, (c + 1) * width)
            out.append(_dot(act, wd_bf[:, cols]) + bd_ref[0, :, cols])
            issue_some()
        assert not starts
        before_store()
        _store_token_major(ybuf.at[slot], jnp.concatenate(out, axis=-1))

    @pl.when(j < nvalid)
    def _():
        e_prev = te_ref[jnp.maximum(j - 1, 0)]
        changed = jnp.logical_or(j == 0, e_prev != te_ref[j])

        @pl.when(changed)
        def _():
            wg_bf[...] = wg_ref[0].astype(BF16)
            wu_bf[...] = wu_ref[0].astype(BF16)
            wd_bf[...] = wd_ref[0].astype(BF16)

        wait_gather(xslot)
        steady = jnp.logical_and(jnp.logical_and(j >= 2, j + GATHER_AHEAD < nvalid),
                                 cnt_ref[jnp.maximum(j - 1, 0)] == MOE_TILE)

        @pl.when(steady)
        def _():
            starts = []
            for r in range(MOE_TILE):
                starts.append(gather_copy(ahead_ref, r, ahead_slot).start)
                starts.append(functools.partial(scatter_copy(r, 1 - slot).start, priority=1))
            compute(lambda: wait_scatter(j - 2), starts)

        @pl.when(jnp.logical_not(steady))
        def _():
            @pl.when(j + GATHER_AHEAD < nvalid)
            def _():
                start_gather(ahead_ref, ahead_slot)

            @pl.when(j >= 1)
            def _():
                start_scatter(j - 1)

            def retire():
                @pl.when(j >= 2)
                def _():
                    wait_scatter(j - 2)
            compute(retire)

    @pl.when(j == nvalid)
    def _():
        start_scatter(j - 1)

    @pl.when(j == nt - 1)
    def _():
        wait_scatter(nvalid - 1)

        @pl.when(nvalid >= 2)
        def _():
            wait_scatter(nvalid - 2)


def _combine_kernel(h_ref, y0_ref, y1_ref, y2_ref, y3_ref, w_ref, g_ref, o_ref):
    w = w_ref[...]
    acc = h_ref[...]
    for kk, y_ref in enumerate((y0_ref, y1_ref, y2_ref, y3_ref)):
        acc = acc + w[:, kk:kk + 1] * _load_token_major(y_ref, COMBINE_TILE)
    o_ref[...] = _rms(acc, g_ref[...])


def kernel(x_prompt, x_sample, state_ret, cache_conv, norm_mix_g, w_in, conv_w, conv_b, conv_ln_g,
           conv_ln_b, ret_norm_g, w_out, norm_ffn_g, w_router, b_router, w_e_gate, b_e_gate, w_e_up,
           b_e_up, w_e_down, b_e_down, norm_final_g):
    bp, lp, _ = x_prompt.shape
    bs, ls, _ = x_sample.shape
    depth = norm_mix_g.shape[0]
    assert depth == 1 and lp % TILE == 0 and TILE % ls == 0 and (bs * ls) % TILE == 0
    np_tok, ns_tok = bp * lp, bs * ls
    n_tok = np_tok + ns_tok
    nseg = SUB // ls

    row = lambda t: t.reshape(1, -1)
    wr_hi = w_router[0].astype(BF16)
    wr_lo = (w_router[0] - wr_hi.astype(F32)).astype(BF16)
    wr_cat = jnp.concatenate(
        [wr_hi, wr_lo, jnp.zeros((D_MODEL, LANES - 2 * N_EXPERTS), BF16)], axis=1).T
    br_bcast = jnp.broadcast_to(b_router[0][:, None], (N_EXPERTS, SUB))
    weights = (row(norm_mix_g[0]), w_in[0].astype(BF16), conv_w[0], row(conv_b[0]), row(conv_ln_g[0]),
               row(conv_ln_b[0]), row(ret_norm_g[0]), w_out[0].astype(BF16), row(norm_ffn_g[0]),
               wr_cat, br_bcast)

    sign = _rope_sign()
    cos_b, sin_b = _rope_cos_sin(jnp.arange(lp // SUB, dtype=F32) * SUB)
    cos_r, sin_r = _rope_cos_sin(jnp.arange(SUB, dtype=F32))
    rope_p = (cos_b, sin_b, cos_r, sin_r, cos_r * sign, sin_r * sign)
    cos_s, sin_s = _rope_cos_sin(jnp.arange(ls, dtype=F32) + PAST_LEN)
    cos_s, sin_s = jnp.tile(cos_s, (nseg, 1)), jnp.tile(sin_s * sign, (nseg, 1))
    tabs_p = _retention_tables(SUB, 1, CHUNK)
    tabs_s = _retention_tables(ls, nseg, min(CHUNK, ls))
    tiles_seq = lp // TILE
    n_ptiles = np_tok // TILE
    n_stiles = ns_tok // TILE
    p_tile = lambda i: jnp.minimum(i, n_ptiles - 1)
    s_tile = lambda i: jnp.maximum(i - n_ptiles, 0)
    tok_spec = lambda w: pl.BlockSpec((TILE, w), lambda i: (i, 0))
    tab_specs = [_const_spec((RET_HEADS, SUB, SUB)), _const_spec((RET_HEADS, SUB, RET_DK)),
                 _const_spec((RET_HEADS, SUB, RET_DK)), _const_spec((RET_HEADS, 8, RET_DK))]
    tile_seqs = SUBTILES * nseg
    state_s_spec = pl.BlockSpec((1, tile_seqs, RET_HEADS, RET_DK, RET_DK),
                                lambda i: (0, s_tile(i), 0, 0, 0))
    cache_s_spec = pl.BlockSpec((1, tile_seqs, CONV_K - 1, CONV_WIDTH), lambda i: (0, s_tile(i), 0, 0))
    h_all, hn_all, idx_t, wts_t, cnt_all, state_p, cache_p, state_s, cache_s = pl.pallas_call(
        functools.partial(_mixer_kernel, n_ptiles, tiles_seq, nseg, ls),
        grid=(n_ptiles + n_stiles,),
        in_specs=[pl.BlockSpec((TILE, D_MODEL), lambda i: (p_tile(i), 0)),
                  pl.BlockSpec((TILE, D_MODEL), lambda i: (s_tile(i), 0)),
                  _const_spec((lp // SUB, RET_DK)), _const_spec((lp // SUB, RET_DK))]
                 + [_const_spec((SUB, RET_DK))] * 4
                 + [_const_spec((SUB, RET_DK)), _const_spec((SUB, RET_DK))]
                 + tab_specs + tab_specs + [state_s_spec, cache_s_spec] + _weight_specs(),
        out_specs=[tok_spec(D_MODEL), pl.BlockSpec((TILE * TOKEN_ROWS, LANES), lambda i: (i, 0)),
                   pl.BlockSpec((TOP_K, TILE), lambda i: (0, i)),
                   pl.BlockSpec((TOP_K, TILE), lambda i: (0, i)),
                   pl.BlockSpec((N_EXPERTS, LANES), lambda i: (0, 0)),
                   pl.BlockSpec((1, 1, RET_HEADS, RET_DK, RET_DK),
                                lambda i: (0, p_tile(i) // tiles_seq, 0, 0, 0)),
                   pl.BlockSpec((1, 1, CONV_K - 1, CONV_WIDTH),
                                lambda i: (0, p_tile(i) // tiles_seq, 0, 0)),
                   state_s_spec, cache_s_spec],
        out_shape=[jax.ShapeDtypeStruct((n_tok, D_MODEL), F32),
                   jax.ShapeDtypeStruct((n_tok * TOKEN_ROWS, LANES), F32),
                   jax.ShapeDtypeStruct((TOP_K, n_tok), jnp.int32),
                   jax.ShapeDtypeStruct((TOP_K, n_tok), F32),
                   jax.ShapeDtypeStruct((N_EXPERTS, LANES), F32),
                   jax.ShapeDtypeStruct((1, bp, RET_HEADS, RET_DK, RET_DK), F32),
                   jax.ShapeDtypeStruct((1, bp, CONV_K - 1, CONV_WIDTH), F32),
                   jax.ShapeDtypeStruct(state_ret.shape, F32),
                   jax.ShapeDtypeStruct(cache_conv.shape, F32)],
        scratch_shapes=[pltpu.VMEM((SUBTILES, CONV_PAD + SUB, CONV_WIDTH), F32),
                        pltpu.VMEM((RET_HEADS, RET_DK, RET_DK), F32)],
        compiler_params=pltpu.CompilerParams(dimension_semantics=("arbitrary",),
                                             vmem_limit_bytes=MIXER_VMEM_LIMIT),
        name="mixer",
    )(x_prompt.reshape(np_tok, D_MODEL), x_sample.reshape(ns_tok, D_MODEL), *rope_p, cos_s, sin_s,
      *tabs_p, *tabs_s, state_ret, cache_conv, *weights)

    n_pair = n_tok * TOP_K
    n_tiles = n_pair // MOE_TILE + N_EXPERTS
    n_slot = n_tiles * MOE_TILE
    e_flat = idx_t.reshape(n_pair)
    counts = cnt_all[:, 0].astype(jnp.int32)
    tiles_e = (counts + MOE_TILE - 1) // MOE_TILE
    tile_end = jnp.cumsum(tiles_e)
    n_valid = tile_end[-1]
    tile_start = tile_end - tiles_e
    pair_bits = (n_pair - 1).bit_length()
    pad_mark = (1 << pair_bits) - 1
    assert pad_mark >= n_pair and (N_EXPERTS + 1) << pair_bits < 2 ** 31
    pad_end = jnp.cumsum(tiles_e * MOE_TILE - counts)
    pad_id = jnp.arange(n_slot - n_pair, dtype=jnp.int32)
    pad_e = jnp.sum((pad_end[None, :] <= pad_id[:, None]).astype(jnp.int32), axis=1)
    pair = jnp.arange(n_pair, dtype=jnp.int32)
    keys = jnp.concatenate([(e_flat << pair_bits) | pair, (pad_e << pair_bits) | pad_mark])
    slot_pair = lax.sort(keys, is_stable=False) & pad_mark
    slot_pair = jnp.where(slot_pair == pad_mark, 0, slot_pair)
    slot_src = (slot_pair % n_tok) * TOKEN_ROWS
    slot_dst = slot_pair * TOKEN_ROWS
    tile_id = jnp.arange(n_tiles, dtype=jnp.int32)
    tile_q = jnp.minimum(tile_id, n_valid - 1)
    tile_e = jnp.sum((tile_end[None, :] <= tile_q[:, None]).astype(jnp.int32), axis=1)
    tile_e = jnp.minimum(tile_e, N_EXPERTS - 1).astype(jnp.int32)
    of_tile = (tile_e[:, None] == jnp.arange(N_EXPERTS, dtype=jnp.int32)[None, :]).astype(jnp.int32)
    rows_left = jnp.sum(of_tile * (counts + tile_start * MOE_TILE)[None, :], axis=1) - tile_id * MOE_TILE
    tile_cnt = jnp.where(tile_id < n_valid, jnp.clip(rows_left, 0, MOE_TILE), 0).astype(jnp.int32)
    slot_src = slot_src.reshape(n_tiles, 1, MOE_TILE)
    slot_dst = slot_dst.reshape(n_tiles, 1, MOE_TILE)

    any_spec = pl.BlockSpec(memory_space=pl.ANY)
    smem_spec = lambda fn: pl.BlockSpec((1, 1, MOE_TILE), fn, memory_space=pltpu.SMEM)
    w_spec = pl.BlockSpec((1, D_MODEL, D_MODEL), lambda j, te, cnt, nv: (te[j], 0, 0))
    b_spec = pl.BlockSpec((1, 1, D_MODEL), lambda j, te, cnt, nv: (te[j], 0, 0))
    y_rows = pl.pallas_call(
        _moe_kernel,
        grid_spec=pltpu.PrefetchScalarGridSpec(
            num_scalar_prefetch=3,
            grid=(n_tiles,),
            in_specs=[pl.BlockSpec((GATHER_AHEAD, 1, MOE_TILE), lambda j, te, cnt, nv: (0, 0, 0),
                                   memory_space=pltpu.SMEM),
                      smem_spec(lambda j, te, cnt, nv: (jnp.minimum(j + GATHER_AHEAD, n_tiles - 1), 0, 0)),
                      smem_spec(lambda j, te, cnt, nv: (jnp.maximum(j - 1, 0), 0, 0)),
                      any_spec, w_spec, w_spec, w_spec, b_spec, b_spec, b_spec],
            out_specs=any_spec,
            scratch_shapes=[pltpu.VMEM((X_BUFFERS, MOE_TILE * TOKEN_ROWS, LANES), F32),
                            pltpu.VMEM((2, MOE_TILE * TOKEN_ROWS, LANES), F32),
                            pltpu.VMEM((D_MODEL, D_MODEL), BF16),
                            pltpu.VMEM((D_MODEL, D_MODEL), BF16),
                            pltpu.VMEM((D_MODEL, D_MODEL), BF16),
                            pltpu.SemaphoreType.DMA((X_BUFFERS,)),
                            pltpu.SemaphoreType.DMA((2,))]),
        out_shape=jax.ShapeDtypeStruct((n_pair * TOKEN_ROWS, LANES), F32),
        compiler_params=pltpu.CompilerParams(dimension_semantics=("arbitrary",),
                                             vmem_limit_bytes=VMEM_LIMIT),
        name="moe_experts",
    )(tile_e, tile_cnt, n_valid.reshape(1).astype(jnp.int32), slot_src, slot_src, slot_dst, hn_all,
      w_e_gate[0], w_e_up[0], w_e_down[0],
      b_e_gate[0].reshape(N_EXPERTS, 1, D_MODEL), b_e_up[0].reshape(N_EXPERTS, 1, D_MODEL),
      b_e_down[0].reshape(N_EXPERTS, 1, D_MODEL))

    def combine(row0, rows):
        ct = COMBINE_TILE
        blk0 = row0 // ct
        planes = [pl.BlockSpec((ct * TOKEN_ROWS, LANES), functools.partial(
            lambda i, kk: (kk * (n_tok // ct) + blk0 + i, 0), kk=kk)) for kk in range(TOP_K)]
        return pl.pallas_call(
            _combine_kernel,
            grid=(rows // ct,),
            in_specs=[pl.BlockSpec((ct, D_MODEL), lambda i: (blk0 + i, 0))] + planes
                     + [pl.BlockSpec((ct, TOP_K), lambda i: (blk0 + i, 0)), _const_spec((1, D_MODEL))],
            out_specs=pl.BlockSpec((ct, D_MODEL), lambda i: (i, 0)),
            out_shape=jax.ShapeDtypeStruct((rows, D_MODEL), F32),
            compiler_params=pltpu.CompilerParams(dimension_semantics=("arbitrary",),
                                                 vmem_limit_bytes=VMEM_LIMIT),
            name="combine",
        )(h_all, y_rows, y_rows, y_rows, y_rows, wts_all, row(norm_final_g))

    wts_all = wts_t.T

    y_prompt = combine(0, np_tok).reshape(bp, lp, D_MODEL)
    y_sample = combine(np_tok, ns_tok).reshape(bs, ls, D_MODEL)
    return (y_prompt, y_sample, state_p, cache_p, state_s, cache_s)
```

```python
import functools

import jax
import jax.numpy as jnp
from jax import lax
from jax.experimental import pallas as pl
from jax.experimental.pallas import tpu as pltpu

D_MODEL = 1024
CHUNK = 64
RET_WIDTH = 512
RET_HEADS = 4
RET_DK = 128
CONV_WIDTH = 512
CONV_K = 31
IN_COLS = 4 * RET_WIDTH + 2 * CONV_WIDTH
ROPE_BASE = 10000.0
N_EXPERTS = 32
TOP_K = 4
SWIGLU_LIMIT = 7.0
SWIGLU_ALPHA = 1.702
EPS = 1e-6
PAST_LEN = 2048

LANES = 128
SUBLANES = 8
SUB = 256
SUBTILES = 2
TILE = SUB * SUBTILES
COMBINE_TILE = 256
CONV_PAD = 32
CONV_ROWS = 32
MOE_TILE = 256
DMA_UNROLL = 8
FFN_CHUNKS = 4
GATHER_AHEAD = 2
X_BUFFERS = GATHER_AHEAD + 1
VMEM_LIMIT = 52 * 1024 * 1024
MIXER_VMEM_LIMIT = 62 * 1024 * 1024

F32 = jnp.float32
BF16 = jnp.bfloat16


def _dot(a, b):
    return jnp.dot(a, b, preferred_element_type=F32)


def _dot_nt(a, b):
    return lax.dot_general(a, b, (((1,), (1,)), ((), ())), preferred_element_type=F32)


def _dot_tn(a, b):
    return lax.dot_general(a, b, (((0,), (0,)), ((), ())), preferred_element_type=F32)


def _rms(x, g):
    return x * lax.rsqrt(jnp.mean(x * x, axis=-1, keepdims=True) + EPS) * g


def _sigmoid(x):
    return 1.0 / (1.0 + jnp.exp(-x))


TOKEN_ROWS = D_MODEL // LANES
assert TOKEN_ROWS == SUBLANES


def _store_token_major(ref, val):
    t = val.shape[0]
    for c in range(TOKEN_ROWS):
        ref[pl.ds(c, t, stride=TOKEN_ROWS), :] = val[:, c * LANES:(c + 1) * LANES]


def _load_token_major(ref, t):
    return jnp.concatenate([ref[pl.ds(c, t, stride=TOKEN_ROWS), :] for c in range(TOKEN_ROWS)], axis=-1)


def _project(x_ref, gmix_ref, win_ref):
    return _dot(_rms(x_ref[...], gmix_ref[...]).astype(BF16), win_ref[...])


def _mixer_body(nseg, seg, first, x_ref, cosf, sinf, mask_ref, qdec_ref, kdec_ref, cdec_ref,
                s_in, c_in, gmix_ref, win_ref, convw_ref, convb_ref, lng_ref, lnb_ref, retg_ref,
                wout_ref, gffn_ref, wr_ref, br_ref,
                h_ref, hn_ref, idx_ref, wts_ref, cnt_ref, s_out, c_out, ubuf, s_scr,
                seg0=0, ubuf_prev=None, proj=None):
    carry = first is not None
    T = nseg * seg
    x = x_ref[...]

    def fresh(val):
        return val if first is False else jnp.where(first, 0.0, val)

    if carry:
        ubuf[0:CONV_PAD, :] = fresh(ubuf_prev[seg:seg + CONV_PAD, :])

    o_heads = []
    for h in range(RET_HEADS):
        q = proj[:, h * RET_DK:(h + 1) * RET_DK]
        k = proj[:, RET_WIDTH + h * RET_DK:RET_WIDTH + (h + 1) * RET_DK]
        v = proj[:, 2 * RET_WIDTH + h * RET_DK:2 * RET_WIDTH + (h + 1) * RET_DK].astype(BF16)
        q = q * cosf + pltpu.roll(q, RET_DK // 2, 1) * sinf
        k = (k * cosf + pltpu.roll(k, RET_DK // 2, 1) * sinf) * (RET_DK ** -0.5)
        p = (_dot_nt(q.astype(BF16), k.astype(BF16)) * mask_ref[h]).astype(BF16)
        o = _dot(p, v)
        qd = (q * qdec_ref[h]).astype(BF16)
        kd = (k * kdec_ref[h]).astype(BF16)
        cdec = cdec_ref[h, 0:1, :]
        o_cross = []
        for s in range(nseg):
            rows = slice(s * seg, (s + 1) * seg)
            s_prev = fresh(s_scr[h]) if carry else s_in[0, seg0 + s, h]
            o_cross.append(_dot(qd[rows], s_prev.astype(BF16)))
            s_new = cdec * s_prev + _dot_tn(kd[rows], v[rows])
            if carry:
                s_scr[h] = s_new
                s_out[0, 0, h] = s_new
            else:
                s_out[0, seg0 + s, h] = s_new
        o = o + (o_cross[0] if nseg == 1 else jnp.concatenate(o_cross, axis=0))
        o = o * lax.rsqrt(jnp.mean(o * o, axis=-1, keepdims=True) + EPS)
        g = proj[:, 3 * RET_WIDTH + h * RET_DK:3 * RET_WIDTH + (h + 1) * RET_DK]
        o_heads.append(o * retg_ref[:, h * RET_DK:(h + 1) * RET_DK] * (g * _sigmoid(g)))

    a = proj[:, 4 * RET_WIDTH:4 * RET_WIDTH + CONV_WIDTH]
    ga = proj[:, 4 * RET_WIDTH + CONV_WIDTH:]
    u = a * _sigmoid(ga)
    hist = CONV_PAD - (CONV_K - 1)
    cv_chunks = []
    for s in range(nseg):
        if not carry:
            ubuf[hist:CONV_PAD, :] = c_in[0, seg0 + s]
        ubuf[CONV_PAD:CONV_PAD + seg, :] = u[s * seg:(s + 1) * seg]
        for c in range(seg // CONV_ROWS):
            acc = jnp.broadcast_to(convb_ref[...], (CONV_ROWS, CONV_WIDTH))
            for shift in range(SUBLANES):
                rows = CONV_ROWS + (SUBLANES if shift else 0)
                part = None
                for j in range(CONV_K):
                    if (hist + j) % SUBLANES == shift:
                        r0 = c * CONV_ROWS + (hist + j) - shift
                        term = convw_ref[j:j + 1, :] * ubuf[r0:r0 + rows, :]
                        part = term if part is None else part + term
                acc = acc + part[shift:shift + CONV_ROWS, :]
            cv_chunks.append(acc)
        tail = ubuf[hist + seg:CONV_PAD + seg, :]
        if carry:
            c_out[0, 0] = tail
        else:
            c_out[0, seg0 + s] = tail
    cv = jnp.concatenate(cv_chunks, axis=0)
    mu = jnp.mean(cv, axis=-1, keepdims=True)
    xc = cv - mu
    ln = xc * lax.rsqrt(jnp.mean(xc * xc, axis=-1, keepdims=True) + EPS) * lng_ref[...] + lnb_ref[...]
    o_conv = ln * _sigmoid(ln)

    mixed = jnp.concatenate(o_heads + [o_conv], axis=-1).astype(BF16)
    hres = x + _dot(mixed, wout_ref[...])
    h_ref[...] = hres

    hn = _rms(hres, gffn_ref[...])
    _store_token_major(hn_ref, hn)
    hn_hi = hn.astype(BF16)
    hn_lo = (hn - hn_hi.astype(F32)).astype(BF16)
    r_hi = _dot_nt(wr_ref[...], hn_hi)
    r_lo = _dot_nt(wr_ref[...], hn_lo)
    logits = (r_hi[:N_EXPERTS] + r_hi[N_EXPERTS:2 * N_EXPERTS] + r_lo[:N_EXPERTS] + br_ref[...])
    expert = lax.broadcasted_iota(jnp.int32, (N_EXPERTS, T), 0).astype(F32)
    work = logits
    top_v, top_i = [], []
    picked = jnp.zeros((N_EXPERTS, T), F32)
    for kk in range(TOP_K):
        m = jnp.max(work, axis=0, keepdims=True)
        sel = jnp.min(jnp.where(work == m, expert, float(N_EXPERTS)), axis=0, keepdims=True)
        top_v.append(m)
        top_i.append(sel)
        hit = expert == sel
        picked = picked + hit.astype(F32)
        work = jnp.where(hit, -jnp.inf, work)
    ex = jnp.exp(jnp.concatenate(top_v, axis=0) - top_v[0])
    wts_ref[...] = ex / jnp.sum(ex, axis=0, keepdims=True)
    idx_ref[...] = jnp.concatenate(top_i, axis=0).astype(jnp.int32)
    cnt_ref[...] += jnp.broadcast_to(jnp.sum(picked, axis=1, keepdims=True), (N_EXPERTS, LANES))


def _mixer_kernel(n_prompt_tiles, tiles_per_seq, nseg, seg,
                  xp_ref, xs_ref, cb_ref, sb_ref, cr_ref, sr_ref, crs_ref, srs_ref, coss_ref, sins_ref,
                  maskp_ref, qdecp_ref, kdecp_ref, cdecp_ref,
                  masks_ref, qdecs_ref, kdecs_ref, cdecs_ref, s_in, c_in,
                  gmix_ref, win_ref, convw_ref, convb_ref, lng_ref, lnb_ref, retg_ref,
                  wout_ref, gffn_ref, wr_ref, br_ref,
                  h_ref, hn_ref, idx_ref, wts_ref, cnt_ref, sp_out, cp_out, ss_out, cs_out,
                  ubufs, s_scr):
    i = pl.program_id(0)
    weights = (gmix_ref, win_ref, convw_ref, convb_ref, lng_ref, lnb_ref, retg_ref,
               wout_ref, gffn_ref, wr_ref, br_ref)

    def rows(ref, k, per_row=1):
        return ref.at[pl.ds(k * SUB * per_row, SUB * per_row)]

    def outs(k):
        cols = pl.ds(k * SUB, SUB)
        return (rows(h_ref, k), rows(hn_ref, k, TOKEN_ROWS), idx_ref.at[:, cols], wts_ref.at[:, cols],
                cnt_ref)

    @pl.when(i == 0)
    def _():
        cnt_ref[...] = jnp.zeros_like(cnt_ref)
        s_scr[...] = jnp.zeros_like(s_scr)
        ubufs[...] = jnp.zeros_like(ubufs)

    @pl.when(i < n_prompt_tiles)
    def _():
        projs = [_project(rows(xp_ref, k), gmix_ref, win_ref) for k in range(SUBTILES)]
        for k in range(SUBTILES):
            first = (i % tiles_per_seq == 0) if k == 0 else False
            base = pl.ds((i % tiles_per_seq) * SUBTILES + k, 1)
            cos_b, sin_b = cb_ref[base, :], sb_ref[base, :]
            cosf = cos_b * cr_ref[...] - sin_b * sr_ref[...]
            sinf = sin_b * crs_ref[...] + cos_b * srs_ref[...]
            _mixer_body(1, SUB, first, rows(xp_ref, k), cosf, sinf,
                        maskp_ref, qdecp_ref, kdecp_ref, cdecp_ref, None, None, *weights, *outs(k),
                        sp_out, cp_out, ubufs.at[k], s_scr, ubuf_prev=ubufs.at[(k - 1) % SUBTILES],
                        proj=projs[k])

    @pl.when(i >= n_prompt_tiles)
    def _():
        projs = [_project(rows(xs_ref, k), gmix_ref, win_ref) for k in range(SUBTILES)]
        for k in range(SUBTILES):
            _mixer_body(nseg, seg, None, rows(xs_ref, k), coss_ref[...], sins_ref[...],
                        masks_ref, qdecs_ref, kdecs_ref, cdecs_ref, s_in, c_in, *weights, *outs(k),
                        ss_out, cs_out, ubufs.at[k], None, seg0=k * nseg, proj=projs[k])


def _retention_tables(seg, nseg, lc):
    T = seg * nseg
    lg = jnp.log(1.0 - 2.0 ** (-5.0 - jnp.arange(RET_HEADS, dtype=F32)))
    i = jnp.arange(T, dtype=jnp.int32)
    loc = (i % seg).astype(F32)
    same_seg = (i[:, None] // seg) == (i[None, :] // seg)
    causal = ((i[None, :] % seg) // lc) <= ((i[:, None] % seg) // lc)
    dist = jnp.abs(loc[:, None] - loc[None, :])
    mask = jnp.where((same_seg & causal)[None], jnp.exp(lg[:, None, None] * dist[None]), 0.0)
    qdec = jnp.exp(lg[:, None] * (loc[None, :] + 1.0))
    kdec = jnp.exp(lg[:, None] * (seg - 1.0 - loc[None, :]))
    cdec = jnp.exp(lg * seg)
    bc = lambda t: jnp.broadcast_to(t[:, :, None], (RET_HEADS, t.shape[1], RET_DK))
    return (mask.astype(F32), bc(qdec), bc(kdec),
            jnp.broadcast_to(cdec[:, None, None], (RET_HEADS, 8, RET_DK)))


def _rope_cos_sin(pos):
    half = RET_DK // 2
    inv = ROPE_BASE ** (-jnp.arange(half, dtype=F32) / half)
    ang = pos[:, None] * inv[None, :]
    cos, sin = jnp.cos(ang), jnp.sin(ang)
    return jnp.concatenate([cos, cos], axis=-1), jnp.concatenate([sin, sin], axis=-1)


def _rope_sign():
    half = RET_DK // 2
    return jnp.concatenate([-jnp.ones((half,), F32), jnp.ones((half,), F32)])[None, :]


def _const_spec(shape):
    return pl.BlockSpec(shape, lambda *_: (0,) * len(shape), pipeline_mode=pl.Buffered(1))


def _weight_specs():
    return [
        _const_spec((1, D_MODEL)),
        _const_spec((D_MODEL, IN_COLS)),
        _const_spec((CONV_K, CONV_WIDTH)),
        _const_spec((1, CONV_WIDTH)),
        _const_spec((1, CONV_WIDTH)),
        _const_spec((1, CONV_WIDTH)),
        _const_spec((1, RET_WIDTH)),
        _const_spec((D_MODEL, D_MODEL)),
        _const_spec((1, D_MODEL)),
        _const_spec((LANES, D_MODEL)),
        _const_spec((N_EXPERTS, SUB)),
    ]


def _moe_kernel(te_ref, cnt_ref, nv_ref, tgrp_ref, gexp_ref, ngrp_ref, first_ref, ahead_ref, dst_ref,
                hn_any, wg_any, wu_any, wd_any, bg_ref, bu_ref, bd_ref, y_any,
                xbuf, ybuf, w32, wg_bf, wu_bf, wd_bf, gsem, ssem, wsem):
    j = pl.program_id(0)
    nt = pl.num_programs(0)
    nvalid = nv_ref[0]
    slot = j % 2
    xslot = j % X_BUFFERS

    def token_rows(base, n=1):
        if not isinstance(base, int):
            base = pl.multiple_of(base, TOKEN_ROWS)
        return pl.ds(base, n * TOKEN_ROWS)

    def gather_copy(idx_ref, r, sl, t=0):
        return pltpu.make_async_copy(hn_any.at[token_rows(idx_ref[t, 0, r])],
                                     xbuf.at[sl, token_rows(r * TOKEN_ROWS)], gsem.at[sl])

    def scatter_copy(r, sl):
        return pltpu.make_async_copy(ybuf.at[sl, token_rows(r * TOKEN_ROWS)],
                                     y_any.at[token_rows(dst_ref[0, 0, r])], ssem.at[sl])

    def start_gather(idx_ref, sl, t=0):
        def body(r, c):
            gather_copy(idx_ref, r, sl, t).start()
            return c
        lax.fori_loop(0, MOE_TILE, body, 0, unroll=DMA_UNROLL)

    def wait_gather(sl):
        pltpu.make_async_copy(hn_any.at[token_rows(0, MOE_TILE)], xbuf.at[sl], gsem.at[sl]).wait()

    def start_scatter(tile):
        n = cnt_ref[tile]
        groups = n // DMA_UNROLL

        def group(g, c):
            for u in range(DMA_UNROLL):
                scatter_copy(g * DMA_UNROLL + u, tile % 2).start()
            return c
        lax.fori_loop(0, groups, group, 0)

        def single(r, c):
            scatter_copy(r, tile % 2).start()
            return c
        lax.fori_loop(groups * DMA_UNROLL, n, single, 0)

    def wait_scatter(tile):
        n = cnt_ref[tile]
        pltpu.make_async_copy(ybuf.at[tile % 2, token_rows(0, n)], y_any.at[token_rows(0, n)],
                              ssem.at[tile % 2]).wait()

    @pl.when(j == 0)
    def _():
        for t in range(GATHER_AHEAD):
            @pl.when(t < nvalid)
            def _():
                start_gather(first_ref, t, t)

    ahead_slot = (j + GATHER_AHEAD) % X_BUFFERS

    def compute(before_store, dma_starts=()):
        starts = list(dma_starts)
        per_chunk = -(-len(starts) // (3 * FFN_CHUNKS))
        width = D_MODEL // FFN_CHUNKS

        def issue_some():
            for _ in range(min(per_chunk, len(starts))):
                starts.pop(0)()

        x = _load_token_major(xbuf.at[xslot], MOE_TILE).astype(BF16)
        gate, act, out = [], [], []
        for c in range(FFN_CHUNKS):
            cols = slice(c * width, (c + 1) * width)
            gate.append(jnp.minimum(_dot(x, wg_bf[:, cols]) + bg_ref[0, :, cols], SWIGLU_LIMIT))
            issue_some()
        for c in range(FFN_CHUNKS):
            cols = slice(c * width, (c + 1) * width)
            up = jnp.clip(_dot(x, wu_bf[:, cols]) + bu_ref[0, :, cols], -SWIGLU_LIMIT, SWIGLU_LIMIT)
            act.append((gate[c] * _sigmoid(SWIGLU_ALPHA * gate[c]) * (up + 1.0)).astype(BF16))
            issue_some()
        act = jnp.concatenate(act, axis=-1)
        for c in range(FFN_CHUNKS):
            cols = slice(c * width, (c + 1) * width)
            out.append(_dot(act, wd_bf[:, cols]) + bd_ref[0, :, cols])
            issue_some()
        assert not starts
        before_store()
        _store_token_major(ybuf.at[slot], jnp.concatenate(out, axis=-1))

    @pl.when(j < nvalid)
    def _():
        group = tgrp_ref[j]
        half = group % 2

        def weight_copies(expert, dst_half):
            return [pltpu.make_async_copy(w_any.at[expert], w32.at[dst_half, m], wsem.at[dst_half])
                    for m, w_any in enumerate((wg_any, wu_any, wd_any))]

        @pl.when(jnp.logical_or(j == 0, tgrp_ref[jnp.maximum(j - 1, 0)] != group))
        def _():
            @pl.when(j == 0)
            def _():
                for copy in weight_copies(gexp_ref[0], 0):
                    copy.start()

            for copy in weight_copies(te_ref[j], half):
                copy.wait()
            for m, w_bf in enumerate((wg_bf, wu_bf, wd_bf)):
                w_bf[...] = w32[half, m].astype(BF16)

            @pl.when(group + 1 < ngrp_ref[0])
            def _():
                for copy in weight_copies(gexp_ref[group + 1], 1 - half):
                    copy.start()

        wait_gather(xslot)
        steady = jnp.logical_and(jnp.logical_and(j >= 2, j + GATHER_AHEAD < nvalid),
                                 cnt_ref[jnp.maximum(j - 1, 0)] == MOE_TILE)

        @pl.when(steady)
        def _():
            starts = []
            for r in range(MOE_TILE):
                starts.append(gather_copy(ahead_ref, r, ahead_slot).start)
                starts.append(functools.partial(scatter_copy(r, 1 - slot).start, priority=1))
            compute(lambda: wait_scatter(j - 2), starts)

        @pl.when(jnp.logical_not(steady))
        def _():
            @pl.when(j + GATHER_AHEAD < nvalid)
            def _():
                start_gather(ahead_ref, ahead_slot)

            @pl.when(j >= 1)
            def _():
                start_scatter(j - 1)

            def retire():
                @pl.when(j >= 2)
                def _():
                    wait_scatter(j - 2)
            compute(retire)

    @pl.when(j == nvalid)
    def _():
        start_scatter(j - 1)

    @pl.when(j == nt - 1)
    def _():
        wait_scatter(nvalid - 1)

        @pl.when(nvalid >= 2)
        def _():
            wait_scatter(nvalid - 2)


def _combine_kernel(h_ref, y0_ref, y1_ref, y2_ref, y3_ref, w_ref, g_ref, o_ref):
    w = w_ref[...]
    acc = h_ref[...]
    for kk, y_ref in enumerate((y0_ref, y1_ref, y2_ref, y3_ref)):
        acc = acc + w[:, kk:kk + 1] * _load_token_major(y_ref, COMBINE_TILE)
    o_ref[...] = _rms(acc, g_ref[...])


def kernel(x_prompt, x_sample, state_ret, cache_conv, norm_mix_g, w_in, conv_w, conv_b, conv_ln_g,
           conv_ln_b, ret_norm_g, w_out, norm_ffn_g, w_router, b_router, w_e_gate, b_e_gate, w_e_up,
           b_e_up, w_e_down, b_e_down, norm_final_g):
    bp, lp, _ = x_prompt.shape
    bs, ls, _ = x_sample.shape
    depth = norm_mix_g.shape[0]
    assert depth == 1 and lp % TILE == 0 and TILE % ls == 0 and (bs * ls) % TILE == 0
    np_tok, ns_tok = bp * lp, bs * ls
    n_tok = np_tok + ns_tok
    nseg = SUB // ls

    row = lambda t: t.reshape(1, -1)
    wr_hi = w_router[0].astype(BF16)
    wr_lo = (w_router[0] - wr_hi.astype(F32)).astype(BF16)
    wr_cat = jnp.concatenate(
        [wr_hi, wr_lo, jnp.zeros((D_MODEL, LANES - 2 * N_EXPERTS), BF16)], axis=1).T
    br_bcast = jnp.broadcast_to(b_router[0][:, None], (N_EXPERTS, SUB))
    weights = (row(norm_mix_g[0]), w_in[0].astype(BF16), conv_w[0], row(conv_b[0]), row(conv_ln_g[0]),
               row(conv_ln_b[0]), row(ret_norm_g[0]), w_out[0].astype(BF16), row(norm_ffn_g[0]),
               wr_cat, br_bcast)

    sign = _rope_sign()
    cos_b, sin_b = _rope_cos_sin(jnp.arange(lp // SUB, dtype=F32) * SUB)
    cos_r, sin_r = _rope_cos_sin(jnp.arange(SUB, dtype=F32))
    rope_p = (cos_b, sin_b, cos_r, sin_r, cos_r * sign, sin_r * sign)
    cos_s, sin_s = _rope_cos_sin(jnp.arange(ls, dtype=F32) + PAST_LEN)
    cos_s, sin_s = jnp.tile(cos_s, (nseg, 1)), jnp.tile(sin_s * sign, (nseg, 1))
    tabs_p = _retention_tables(SUB, 1, CHUNK)
    tabs_s = _retention_tables(ls, nseg, min(CHUNK, ls))
    tiles_seq = lp // TILE
    n_ptiles = np_tok // TILE
    n_stiles = ns_tok // TILE
    p_tile = lambda i: jnp.minimum(i, n_ptiles - 1)
    s_tile = lambda i: jnp.maximum(i - n_ptiles, 0)
    tok_spec = lambda w: pl.BlockSpec((TILE, w), lambda i: (i, 0))
    tab_specs = [_const_spec((RET_HEADS, SUB, SUB)), _const_spec((RET_HEADS, SUB, RET_DK)),
                 _const_spec((RET_HEADS, SUB, RET_DK)), _const_spec((RET_HEADS, 8, RET_DK))]
    tile_seqs = SUBTILES * nseg
    state_s_spec = pl.BlockSpec((1, tile_seqs, RET_HEADS, RET_DK, RET_DK),
                                lambda i: (0, s_tile(i), 0, 0, 0))
    cache_s_spec = pl.BlockSpec((1, tile_seqs, CONV_K - 1, CONV_WIDTH), lambda i: (0, s_tile(i), 0, 0))
    h_all, hn_all, idx_t, wts_t, cnt_all, state_p, cache_p, state_s, cache_s = pl.pallas_call(
        functools.partial(_mixer_kernel, n_ptiles, tiles_seq, nseg, ls),
        grid=(n_ptiles + n_stiles,),
        in_specs=[pl.BlockSpec((TILE, D_MODEL), lambda i: (p_tile(i), 0)),
                  pl.BlockSpec((TILE, D_MODEL), lambda i: (s_tile(i), 0)),
                  _const_spec((lp // SUB, RET_DK)), _const_spec((lp // SUB, RET_DK))]
                 + [_const_spec((SUB, RET_DK))] * 4
                 + [_const_spec((SUB, RET_DK)), _const_spec((SUB, RET_DK))]
                 + tab_specs + tab_specs + [state_s_spec, cache_s_spec] + _weight_specs(),
        out_specs=[tok_spec(D_MODEL), pl.BlockSpec((TILE * TOKEN_ROWS, LANES), lambda i: (i, 0)),
                   pl.BlockSpec((TOP_K, TILE), lambda i: (0, i)),
                   pl.BlockSpec((TOP_K, TILE), lambda i: (0, i)),
                   pl.BlockSpec((N_EXPERTS, LANES), lambda i: (0, 0)),
                   pl.BlockSpec((1, 1, RET_HEADS, RET_DK, RET_DK),
                                lambda i: (0, p_tile(i) // tiles_seq, 0, 0, 0)),
                   pl.BlockSpec((1, 1, CONV_K - 1, CONV_WIDTH),
                                lambda i: (0, p_tile(i) // tiles_seq, 0, 0)),
                   state_s_spec, cache_s_spec],
        out_shape=[jax.ShapeDtypeStruct((n_tok, D_MODEL), F32),
                   jax.ShapeDtypeStruct((n_tok * TOKEN_ROWS, LANES), F32),
                   jax.ShapeDtypeStruct((TOP_K, n_tok), jnp.int32),
                   jax.ShapeDtypeStruct((TOP_K, n_tok), F32),
                   jax.ShapeDtypeStruct((N_EXPERTS, LANES), F32),
                   jax.ShapeDtypeStruct((1, bp, RET_HEADS, RET_DK, RET_DK), F32),
                   jax.ShapeDtypeStruct((1, bp, CONV_K - 1, CONV_WIDTH), F32),
                   jax.ShapeDtypeStruct(state_ret.shape, F32),
                   jax.ShapeDtypeStruct(cache_conv.shape, F32)],
        scratch_shapes=[pltpu.VMEM((SUBTILES, CONV_PAD + SUB, CONV_WIDTH), F32),
                        pltpu.VMEM((RET_HEADS, RET_DK, RET_DK), F32)],
        compiler_params=pltpu.CompilerParams(dimension_semantics=("arbitrary",),
                                             vmem_limit_bytes=MIXER_VMEM_LIMIT),
        name="mixer",
    )(x_prompt.reshape(np_tok, D_MODEL), x_sample.reshape(ns_tok, D_MODEL), *rope_p, cos_s, sin_s,
      *tabs_p, *tabs_s, state_ret, cache_conv, *weights)

    n_pair = n_tok * TOP_K
    n_tiles = n_pair // MOE_TILE + N_EXPERTS
    n_slot = n_tiles * MOE_TILE
    e_flat = idx_t.reshape(n_pair)
    counts = cnt_all[:, 0].astype(jnp.int32)
    tiles_e = (counts + MOE_TILE - 1) // MOE_TILE
    tile_end = jnp.cumsum(tiles_e)
    n_valid = tile_end[-1]
    tile_start = tile_end - tiles_e
    pair_bits = (n_pair - 1).bit_length()
    pad_mark = (1 << pair_bits) - 1
    assert pad_mark >= n_pair and (N_EXPERTS + 1) << pair_bits < 2 ** 31
    pad_end = jnp.cumsum(tiles_e * MOE_TILE - counts)
    pad_id = jnp.arange(n_slot - n_pair, dtype=jnp.int32)
    pad_e = jnp.sum((pad_end[None, :] <= pad_id[:, None]).astype(jnp.int32), axis=1)
    pair = jnp.arange(n_pair, dtype=jnp.int32)
    keys = jnp.concatenate([(e_flat << pair_bits) | pair, (pad_e << pair_bits) | pad_mark])
    slot_pair = lax.sort(keys, is_stable=False) & pad_mark
    slot_pair = jnp.where(slot_pair == pad_mark, 0, slot_pair)
    slot_src = (slot_pair % n_tok) * TOKEN_ROWS
    slot_dst = slot_pair * TOKEN_ROWS
    tile_id = jnp.arange(n_tiles, dtype=jnp.int32)
    tile_q = jnp.minimum(tile_id, n_valid - 1)
    tile_e = jnp.sum((tile_end[None, :] <= tile_q[:, None]).astype(jnp.int32), axis=1)
    tile_e = jnp.minimum(tile_e, N_EXPERTS - 1).astype(jnp.int32)
    of_tile = (tile_e[:, None] == jnp.arange(N_EXPERTS, dtype=jnp.int32)[None, :]).astype(jnp.int32)
    rows_left = jnp.sum(of_tile * (counts + tile_start * MOE_TILE)[None, :], axis=1) - tile_id * MOE_TILE
    tile_cnt = jnp.where(tile_id < n_valid, jnp.clip(rows_left, 0, MOE_TILE), 0).astype(jnp.int32)
    slot_src = slot_src.reshape(n_tiles, 1, MOE_TILE)
    slot_dst = slot_dst.reshape(n_tiles, 1, MOE_TILE)
    present = (counts > 0).astype(jnp.int32)
    grp_of_e = jnp.cumsum(present) - 1
    n_grp = grp_of_e[-1] + 1
    grp_id = jnp.arange(N_EXPERTS + 1, dtype=jnp.int32)
    expert_id = jnp.arange(N_EXPERTS, dtype=jnp.int32)
    grp_expert = jnp.sum(jnp.where((grp_of_e[None, :] == grp_id[:, None]) & (present[None, :] > 0),
                                   expert_id[None, :], 0), axis=1).astype(jnp.int32)
    tile_grp = jnp.sum(of_tile * grp_of_e[None, :], axis=1).astype(jnp.int32)

    any_spec = pl.BlockSpec(memory_space=pl.ANY)
    smem_spec = lambda fn: pl.BlockSpec((1, 1, MOE_TILE), fn, memory_space=pltpu.SMEM)
    b_spec = pl.BlockSpec((1, 1, D_MODEL), lambda j, te, *_: (te[j], 0, 0))
    y_rows = pl.pallas_call(
        _moe_kernel,
        grid_spec=pltpu.PrefetchScalarGridSpec(
            num_scalar_prefetch=6,
            grid=(n_tiles,),
            in_specs=[pl.BlockSpec((GATHER_AHEAD, 1, MOE_TILE), lambda j, *_: (0, 0, 0),
                                   memory_space=pltpu.SMEM),
                      smem_spec(lambda j, *_: (jnp.minimum(j + GATHER_AHEAD, n_tiles - 1), 0, 0)),
                      smem_spec(lambda j, *_: (jnp.maximum(j - 1, 0), 0, 0)),
                      any_spec, any_spec, any_spec, any_spec, b_spec, b_spec, b_spec],
            out_specs=any_spec,
            scratch_shapes=[pltpu.VMEM((X_BUFFERS, MOE_TILE * TOKEN_ROWS, LANES), F32),
                            pltpu.VMEM((2, MOE_TILE * TOKEN_ROWS, LANES), F32),
                            pltpu.VMEM((2, 3, D_MODEL, D_MODEL), F32),
                            pltpu.VMEM((D_MODEL, D_MODEL), BF16),
                            pltpu.VMEM((D_MODEL, D_MODEL), BF16),
                            pltpu.VMEM((D_MODEL, D_MODEL), BF16),
                            pltpu.SemaphoreType.DMA((X_BUFFERS,)),
                            pltpu.SemaphoreType.DMA((2,)),
                            pltpu.SemaphoreType.DMA((2,))]),
        out_shape=jax.ShapeDtypeStruct((n_pair * TOKEN_ROWS, LANES), F32),
        compiler_params=pltpu.CompilerParams(dimension_semantics=("arbitrary",),
                                             vmem_limit_bytes=VMEM_LIMIT),
        name="moe_experts",
    )(tile_e, tile_cnt, n_valid.reshape(1).astype(jnp.int32), tile_grp, grp_expert,
      n_grp.reshape(1).astype(jnp.int32), slot_src, slot_src, slot_dst, hn_all,
      w_e_gate[0], w_e_up[0], w_e_down[0],
      b_e_gate[0].reshape(N_EXPERTS, 1, D_MODEL), b_e_up[0].reshape(N_EXPERTS, 1, D_MODEL),
      b_e_down[0].reshape(N_EXPERTS, 1, D_MODEL))

    def combine(row0, rows):
        ct = COMBINE_TILE
        blk0 = row0 // ct
        planes = [pl.BlockSpec((ct * TOKEN_ROWS, LANES), functools.partial(
            lambda i, kk: (kk * (n_tok // ct) + blk0 + i, 0), kk=kk)) for kk in range(TOP_K)]
        return pl.pallas_call(
            _combine_kernel,
            grid=(rows // ct,),
            in_specs=[pl.BlockSpec((ct, D_MODEL), lambda i: (blk0 + i, 0))] + planes
                     + [pl.BlockSpec((ct, TOP_K), lambda i: (blk0 + i, 0)), _const_spec((1, D_MODEL))],
            out_specs=pl.BlockSpec((ct, D_MODEL), lambda i: (i, 0)),
            out_shape=jax.ShapeDtypeStruct((rows, D_MODEL), F32),
            compiler_params=pltpu.CompilerParams(dimension_semantics=("arbitrary",),
                                                 vmem_limit_bytes=VMEM_LIMIT),
            name="combine",
        )(h_all, y_rows, y_rows, y_rows, y_rows, wts_all, row(norm_final_g))

    wts_all = wts_t.T

    y_prompt = combine(0, np_tok).reshape(bp, lp, D_MODEL)
    y_sample = combine(np_tok, ns_tok).reshape(bs, ls, D_MODEL)
    return (y_prompt, y_sample, state_p, cache_p, state_s, cache_s)
```

```python
import functools

import jax
import jax.numpy as jnp
from jax import lax
from jax.experimental import pallas as pl
from jax.experimental.pallas import tpu as pltpu

D_MODEL = 1024
CHUNK = 64
RET_WIDTH = 512
RET_HEADS = 4
RET_DK = 128
CONV_WIDTH = 512
CONV_K = 31
IN_COLS = 4 * RET_WIDTH + 2 * CONV_WIDTH
ROPE_BASE = 10000.0
N_EXPERTS = 32
TOP_K = 4
SWIGLU_LIMIT = 7.0
SWIGLU_ALPHA = 1.702
EPS = 1e-6
PAST_LEN = 2048

LANES = 128
SUBLANES = 8
SUB = 256
SUBTILES = 2
TILE = SUB * SUBTILES
COMBINE_TILE = 512
CONV_PAD = 32
CONV_ROWS = 32
MOE_TILE = 256
DMA_UNROLL = 8
GATHER_AHEAD = 2
X_BUFFERS = GATHER_AHEAD + 1
VMEM_LIMIT = 52 * 1024 * 1024
MIXER_VMEM_LIMIT = 62 * 1024 * 1024

F32 = jnp.float32
BF16 = jnp.bfloat16


def _dot(a, b):
    return jnp.dot(a, b, preferred_element_type=F32)


def _dot_nt(a, b):
    return lax.dot_general(a, b, (((1,), (1,)), ((), ())), preferred_element_type=F32)


def _dot_tn(a, b):
    return lax.dot_general(a, b, (((0,), (0,)), ((), ())), preferred_element_type=F32)


def _rms(x, g):
    return x * lax.rsqrt(jnp.mean(x * x, axis=-1, keepdims=True) + EPS) * g


def _sigmoid(x):
    return 1.0 / (1.0 + jnp.exp(-x))


TOKEN_ROWS = D_MODEL // LANES
assert TOKEN_ROWS == SUBLANES


def _store_token_major(ref, val):
    t = val.shape[0]
    for c in range(TOKEN_ROWS):
        ref[pl.ds(c, t, stride=TOKEN_ROWS), :] = val[:, c * LANES:(c + 1) * LANES]


def _load_token_major(ref, t):
    return jnp.concatenate([ref[pl.ds(c, t, stride=TOKEN_ROWS), :] for c in range(TOKEN_ROWS)], axis=-1)


def _project(x_ref, gmix_ref, win_ref):
    return _dot(_rms(x_ref[...], gmix_ref[...]).astype(BF16), win_ref[...])


def _mixer_body(nseg, seg, first, x_ref, cosf, sinf, mask_ref, qdec_ref, kdec_ref, cdec_ref,
                s_in, c_in, gmix_ref, win_ref, convw_ref, convb_ref, lng_ref, lnb_ref, retg_ref,
                wout_ref, gffn_ref, wr_ref, br_ref,
                h_ref, hn_ref, idx_ref, wts_ref, cnt_ref, s_out, c_out, ubuf, s_scr,
                seg0=0, ubuf_prev=None, proj=None):
    carry = first is not None
    T = nseg * seg
    x = x_ref[...]

    def fresh(val):
        return val if first is False else jnp.where(first, 0.0, val)

    if carry:
        ubuf[0:CONV_PAD, :] = fresh(ubuf_prev[seg:seg + CONV_PAD, :])

    o_heads = []
    for h in range(RET_HEADS):
        q = proj[:, h * RET_DK:(h + 1) * RET_DK]
        k = proj[:, RET_WIDTH + h * RET_DK:RET_WIDTH + (h + 1) * RET_DK]
        v = proj[:, 2 * RET_WIDTH + h * RET_DK:2 * RET_WIDTH + (h + 1) * RET_DK].astype(BF16)
        q = q * cosf + pltpu.roll(q, RET_DK // 2, 1) * sinf
        k = (k * cosf + pltpu.roll(k, RET_DK // 2, 1) * sinf) * (RET_DK ** -0.5)
        p = (_dot_nt(q.astype(BF16), k.astype(BF16)) * mask_ref[h]).astype(BF16)
        o = _dot(p, v)
        qd = (q * qdec_ref[h]).astype(BF16)
        kd = (k * kdec_ref[h]).astype(BF16)
        cdec = cdec_ref[h, 0:1, :]
        o_cross = []
        for s in range(nseg):
            rows = slice(s * seg, (s + 1) * seg)
            s_prev = fresh(s_scr[h]) if carry else s_in[0, seg0 + s, h]
            o_cross.append(_dot(qd[rows], s_prev.astype(BF16)))
            s_new = cdec * s_prev + _dot_tn(kd[rows], v[rows])
            if carry:
                s_scr[h] = s_new
                s_out[0, 0, h] = s_new
            else:
                s_out[0, seg0 + s, h] = s_new
        o = o + (o_cross[0] if nseg == 1 else jnp.concatenate(o_cross, axis=0))
        o = o * lax.rsqrt(jnp.mean(o * o, axis=-1, keepdims=True) + EPS)
        g = proj[:, 3 * RET_WIDTH + h * RET_DK:3 * RET_WIDTH + (h + 1) * RET_DK]
        o_heads.append(o * retg_ref[:, h * RET_DK:(h + 1) * RET_DK] * (g * _sigmoid(g)))

    a = proj[:, 4 * RET_WIDTH:4 * RET_WIDTH + CONV_WIDTH]
    ga = proj[:, 4 * RET_WIDTH + CONV_WIDTH:]
    u = a * _sigmoid(ga)
    hist = CONV_PAD - (CONV_K - 1)
    cv_chunks = []
    for s in range(nseg):
        if not carry:
            ubuf[hist:CONV_PAD, :] = c_in[0, seg0 + s]
        ubuf[CONV_PAD:CONV_PAD + seg, :] = u[s * seg:(s + 1) * seg]
        for c in range(seg // CONV_ROWS):
            acc = jnp.broadcast_to(convb_ref[...], (CONV_ROWS, CONV_WIDTH))
            for shift in range(SUBLANES):
                rows = CONV_ROWS + (SUBLANES if shift else 0)
                part = None
                for j in range(CONV_K):
                    if (hist + j) % SUBLANES == shift:
                        r0 = c * CONV_ROWS + (hist + j) - shift
                        term = convw_ref[j:j + 1, :] * ubuf[r0:r0 + rows, :]
                        part = term if part is None else part + term
                acc = acc + part[shift:shift + CONV_ROWS, :]
            cv_chunks.append(acc)
        tail = ubuf[hist + seg:CONV_PAD + seg, :]
        if carry:
            c_out[0, 0] = tail
        else:
            c_out[0, seg0 + s] = tail
    cv = jnp.concatenate(cv_chunks, axis=0)
    mu = jnp.mean(cv, axis=-1, keepdims=True)
    xc = cv - mu
    ln = xc * lax.rsqrt(jnp.mean(xc * xc, axis=-1, keepdims=True) + EPS) * lng_ref[...] + lnb_ref[...]
    o_conv = ln * _sigmoid(ln)

    mixed = jnp.concatenate(o_heads + [o_conv], axis=-1).astype(BF16)
    hres = x + _dot(mixed, wout_ref[...])
    h_ref[...] = hres

    hn = _rms(hres, gffn_ref[...])
    _store_token_major(hn_ref, hn)
    hn_hi = hn.astype(BF16)
    hn_lo = (hn - hn_hi.astype(F32)).astype(BF16)
    r_hi = _dot_nt(wr_ref[...], hn_hi)
    r_lo = _dot_nt(wr_ref[...], hn_lo)
    logits = (r_hi[:N_EXPERTS] + r_hi[N_EXPERTS:2 * N_EXPERTS] + r_lo[:N_EXPERTS] + br_ref[...])
    expert = lax.broadcasted_iota(jnp.int32, (N_EXPERTS, T), 0).astype(F32)
    work = logits
    top_v, top_i = [], []
    picked = jnp.zeros((N_EXPERTS, T), F32)
    for kk in range(TOP_K):
        m = jnp.max(work, axis=0, keepdims=True)
        sel = jnp.min(jnp.where(work == m, expert, float(N_EXPERTS)), axis=0, keepdims=True)
        top_v.append(m)
        top_i.append(sel)
        hit = expert == sel
        picked = picked + hit.astype(F32)
        work = jnp.where(hit, -jnp.inf, work)
    ex = jnp.exp(jnp.concatenate(top_v, axis=0) - top_v[0])
    wts_ref[...] = ex / jnp.sum(ex, axis=0, keepdims=True)
    idx_ref[...] = jnp.concatenate(top_i, axis=0).astype(jnp.int32)
    cnt_ref[...] += jnp.broadcast_to(jnp.sum(picked, axis=1, keepdims=True), (N_EXPERTS, LANES))


def _mixer_kernel(n_prompt_tiles, tiles_per_seq, nseg, seg,
                  xp_ref, xs_ref, cb_ref, sb_ref, cr_ref, sr_ref, crs_ref, srs_ref, coss_ref, sins_ref,
                  maskp_ref, qdecp_ref, kdecp_ref, cdecp_ref,
                  masks_ref, qdecs_ref, kdecs_ref, cdecs_ref, s_in, c_in,
                  gmix_ref, win_ref, convw_ref, convb_ref, lng_ref, lnb_ref, retg_ref,
                  wout_ref, gffn_ref, wr_ref, br_ref,
                  h_ref, hn_ref, idx_ref, wts_ref, cnt_ref, sp_out, cp_out, ss_out, cs_out,
                  ubufs, s_scr):
    i = pl.program_id(0)
    weights = (gmix_ref, win_ref, convw_ref, convb_ref, lng_ref, lnb_ref, retg_ref,
               wout_ref, gffn_ref, wr_ref, br_ref)

    def rows(ref, k, per_row=1):
        return ref.at[pl.ds(k * SUB * per_row, SUB * per_row)]

    def outs(k):
        cols = pl.ds(k * SUB, SUB)
        return (rows(h_ref, k), rows(hn_ref, k, TOKEN_ROWS), idx_ref.at[:, cols], wts_ref.at[:, cols],
                cnt_ref)

    @pl.when(i == 0)
    def _():
        cnt_ref[...] = jnp.zeros_like(cnt_ref)
        s_scr[...] = jnp.zeros_like(s_scr)
        ubufs[...] = jnp.zeros_like(ubufs)

    @pl.when(i < n_prompt_tiles)
    def _():
        projs = [_project(rows(xp_ref, k), gmix_ref, win_ref) for k in range(SUBTILES)]
        for k in range(SUBTILES):
            first = (i % tiles_per_seq == 0) if k == 0 else False
            base = pl.ds((i % tiles_per_seq) * SUBTILES + k, 1)
            cos_b, sin_b = cb_ref[base, :], sb_ref[base, :]
            cosf = cos_b * cr_ref[...] - sin_b * sr_ref[...]
            sinf = sin_b * crs_ref[...] + cos_b * srs_ref[...]
            _mixer_body(1, SUB, first, rows(xp_ref, k), cosf, sinf,
                        maskp_ref, qdecp_ref, kdecp_ref, cdecp_ref, None, None, *weights, *outs(k),
                        sp_out, cp_out, ubufs.at[k], s_scr, ubuf_prev=ubufs.at[(k - 1) % SUBTILES],
                        proj=projs[k])

    @pl.when(i >= n_prompt_tiles)
    def _():
        projs = [_project(rows(xs_ref, k), gmix_ref, win_ref) for k in range(SUBTILES)]
        for k in range(SUBTILES):
            _mixer_body(nseg, seg, None, rows(xs_ref, k), coss_ref[...], sins_ref[...],
                        masks_ref, qdecs_ref, kdecs_ref, cdecs_ref, s_in, c_in, *weights, *outs(k),
                        ss_out, cs_out, ubufs.at[k], None, seg0=k * nseg, proj=projs[k])


def _retention_tables(seg, nseg, lc):
    T = seg * nseg
    lg = jnp.log(1.0 - 2.0 ** (-5.0 - jnp.arange(RET_HEADS, dtype=F32)))
    i = jnp.arange(T, dtype=jnp.int32)
    loc = (i % seg).astype(F32)
    same_seg = (i[:, None] // seg) == (i[None, :] // seg)
    causal = ((i[None, :] % seg) // lc) <= ((i[:, None] % seg) // lc)
    dist = jnp.abs(loc[:, None] - loc[None, :])
    mask = jnp.where((same_seg & causal)[None], jnp.exp(lg[:, None, None] * dist[None]), 0.0)
    qdec = jnp.exp(lg[:, None] * (loc[None, :] + 1.0))
    kdec = jnp.exp(lg[:, None] * (seg - 1.0 - loc[None, :]))
    cdec = jnp.exp(lg * seg)
    bc = lambda t: jnp.broadcast_to(t[:, :, None], (RET_HEADS, t.shape[1], RET_DK))
    return (mask.astype(F32), bc(qdec), bc(kdec),
            jnp.broadcast_to(cdec[:, None, None], (RET_HEADS, 8, RET_DK)))


def _rope_cos_sin(pos):
    half = RET_DK // 2
    inv = ROPE_BASE ** (-jnp.arange(half, dtype=F32) / half)
    ang = pos[:, None] * inv[None, :]
    cos, sin = jnp.cos(ang), jnp.sin(ang)
    return jnp.concatenate([cos, cos], axis=-1), jnp.concatenate([sin, sin], axis=-1)


def _rope_sign():
    half = RET_DK // 2
    return jnp.concatenate([-jnp.ones((half,), F32), jnp.ones((half,), F32)])[None, :]


def _const_spec(shape):
    return pl.BlockSpec(shape, lambda *_: (0,) * len(shape), pipeline_mode=pl.Buffered(1))


def _weight_specs():
    return [
        _const_spec((1, D_MODEL)),
        _const_spec((D_MODEL, IN_COLS)),
        _const_spec((CONV_K, CONV_WIDTH)),
        _const_spec((1, CONV_WIDTH)),
        _const_spec((1, CONV_WIDTH)),
        _const_spec((1, CONV_WIDTH)),
        _const_spec((1, RET_WIDTH)),
        _const_spec((D_MODEL, D_MODEL)),
        _const_spec((1, D_MODEL)),
        _const_spec((LANES, D_MODEL)),
        _const_spec((N_EXPERTS, SUB)),
    ]


def _moe_kernel(te_ref, cnt_ref, nv_ref, tgrp_ref, gexp_ref, ngrp_ref, first_ref, ahead_ref, dst_ref,
                hn_any, wg_any, wu_any, wd_any, bg_ref, bu_ref, bd_ref, y_any,
                xbuf, ybuf, w32, wg_bf, wu_bf, wd_bf, gsem, ssem, wsem):
    j = pl.program_id(0)
    nt = pl.num_programs(0)
    nvalid = nv_ref[0]
    slot = j % 2
    xslot = j % X_BUFFERS

    def token_rows(base, n=1):
        if not isinstance(base, int):
            base = pl.multiple_of(base, TOKEN_ROWS)
        return pl.ds(base, n * TOKEN_ROWS)

    def gather_copy(idx_ref, r, sl, t=0):
        return pltpu.make_async_copy(hn_any.at[token_rows(idx_ref[t, 0, r])],
                                     xbuf.at[sl, token_rows(r * TOKEN_ROWS)], gsem.at[sl])

    def scatter_copy(r, sl):
        return pltpu.make_async_copy(ybuf.at[sl, token_rows(r * TOKEN_ROWS)],
                                     y_any.at[token_rows(dst_ref[0, 0, r])], ssem.at[sl])

    def start_gather(idx_ref, sl, t=0):
        def body(r, c):
            gather_copy(idx_ref, r, sl, t).start()
            return c
        lax.fori_loop(0, MOE_TILE, body, 0, unroll=DMA_UNROLL)

    def wait_gather(sl):
        pltpu.make_async_copy(hn_any.at[token_rows(0, MOE_TILE)], xbuf.at[sl], gsem.at[sl]).wait()

    def start_scatter(tile):
        n = cnt_ref[tile]
        groups = n // DMA_UNROLL

        def group(g, c):
            for u in range(DMA_UNROLL):
                scatter_copy(g * DMA_UNROLL + u, tile % 2).start()
            return c
        lax.fori_loop(0, groups, group, 0)

        def single(r, c):
            scatter_copy(r, tile % 2).start()
            return c
        lax.fori_loop(groups * DMA_UNROLL, n, single, 0)

    def wait_scatter(tile):
        n = cnt_ref[tile]
        pltpu.make_async_copy(ybuf.at[tile % 2, token_rows(0, n)], y_any.at[token_rows(0, n)],
                              ssem.at[tile % 2]).wait()

    @pl.when(j == 0)
    def _():
        for t in range(GATHER_AHEAD):
            @pl.when(t < nvalid)
            def _():
                start_gather(first_ref, t, t)

    ahead_slot = (j + GATHER_AHEAD) % X_BUFFERS

    def compute(before_store, dma_starts=()):
        x = _load_token_major(xbuf.at[xslot], MOE_TILE).astype(BF16)
        for start in dma_starts:
            start()
        gate = jnp.minimum(_dot(x, wg_bf[...]) + bg_ref[0], SWIGLU_LIMIT)
        up = jnp.clip(_dot(x, wu_bf[...]) + bu_ref[0], -SWIGLU_LIMIT, SWIGLU_LIMIT)
        act = (gate * _sigmoid(SWIGLU_ALPHA * gate) * (up + 1.0)).astype(BF16)
        out = _dot(act, wd_bf[...]) + bd_ref[0]
        before_store()
        _store_token_major(ybuf.at[slot], out)

    @pl.when(j < nvalid)
    def _():
        group = tgrp_ref[j]
        half = group % 2

        def weight_copies(expert, dst_half):
            return [pltpu.make_async_copy(w_any.at[expert], w32.at[dst_half, m], wsem.at[dst_half])
                    for m, w_any in enumerate((wg_any, wu_any, wd_any))]

        @pl.when(jnp.logical_or(j == 0, tgrp_ref[jnp.maximum(j - 1, 0)] != group))
        def _():
            @pl.when(j == 0)
            def _():
                for copy in weight_copies(gexp_ref[0], 0):
                    copy.start()

            for copy in weight_copies(te_ref[j], half):
                copy.wait()
            for m, w_bf in enumerate((wg_bf, wu_bf, wd_bf)):
                w_bf[...] = w32[half, m].astype(BF16)

            @pl.when(group + 1 < ngrp_ref[0])
            def _():
                for copy in weight_copies(gexp_ref[group + 1], 1 - half):
                    copy.start()

        wait_gather(xslot)
        steady = jnp.logical_and(jnp.logical_and(j >= 2, j + GATHER_AHEAD < nvalid),
                                 cnt_ref[jnp.maximum(j - 1, 0)] == MOE_TILE)

        @pl.when(steady)
        def _():
            starts = []
            for r in range(MOE_TILE):
                starts.append(gather_copy(ahead_ref, r, ahead_slot).start)
                starts.append(functools.partial(scatter_copy(r, 1 - slot).start, priority=1))
            compute(lambda: wait_scatter(j - 2), starts)

        @pl.when(jnp.logical_not(steady))
        def _():
            @pl.when(j + GATHER_AHEAD < nvalid)
            def _():
                start_gather(ahead_ref, ahead_slot)

            @pl.when(j >= 1)
            def _():
                start_scatter(j - 1)

            def retire():
                @pl.when(j >= 2)
                def _():
                    wait_scatter(j - 2)
            compute(retire)

    @pl.when(j == nvalid)
    def _():
        start_scatter(j - 1)

    @pl.when(j == nt - 1)
    def _():
        wait_scatter(nvalid - 1)

        @pl.when(nvalid >= 2)
        def _():
            wait_scatter(nvalid - 2)


def _combine_kernel(h_ref, y0_ref, y1_ref, y2_ref, y3_ref, w_ref, g_ref, o_ref):
    w = w_ref[...]
    acc = h_ref[...]
    for kk, y_ref in enumerate((y0_ref, y1_ref, y2_ref, y3_ref)):
        acc = acc + w[:, kk:kk + 1] * _load_token_major(y_ref, COMBINE_TILE)
    o_ref[...] = _rms(acc, g_ref[...])


def kernel(x_prompt, x_sample, state_ret, cache_conv, norm_mix_g, w_in, conv_w, conv_b, conv_ln_g,
           conv_ln_b, ret_norm_g, w_out, norm_ffn_g, w_router, b_router, w_e_gate, b_e_gate, w_e_up,
           b_e_up, w_e_down, b_e_down, norm_final_g):
    bp, lp, _ = x_prompt.shape
    bs, ls, _ = x_sample.shape
    depth = norm_mix_g.shape[0]
    assert depth == 1 and lp % TILE == 0 and TILE % ls == 0 and (bs * ls) % TILE == 0
    np_tok, ns_tok = bp * lp, bs * ls
    n_tok = np_tok + ns_tok
    nseg = SUB // ls

    row = lambda t: t.reshape(1, -1)
    wr_hi = w_router[0].astype(BF16)
    wr_lo = (w_router[0] - wr_hi.astype(F32)).astype(BF16)
    wr_cat = jnp.concatenate(
        [wr_hi, wr_lo, jnp.zeros((D_MODEL, LANES - 2 * N_EXPERTS), BF16)], axis=1).T
    br_bcast = jnp.broadcast_to(b_router[0][:, None], (N_EXPERTS, SUB))
    weights = (row(norm_mix_g[0]), w_in[0].astype(BF16), conv_w[0], row(conv_b[0]), row(conv_ln_g[0]),
               row(conv_ln_b[0]), row(ret_norm_g[0]), w_out[0].astype(BF16), row(norm_ffn_g[0]),
               wr_cat, br_bcast)

    sign = _rope_sign()
    cos_b, sin_b = _rope_cos_sin(jnp.arange(lp // SUB, dtype=F32) * SUB)
    cos_r, sin_r = _rope_cos_sin(jnp.arange(SUB, dtype=F32))
    rope_p = (cos_b, sin_b, cos_r, sin_r, cos_r * sign, sin_r * sign)
    cos_s, sin_s = _rope_cos_sin(jnp.arange(ls, dtype=F32) + PAST_LEN)
    cos_s, sin_s = jnp.tile(cos_s, (nseg, 1)), jnp.tile(sin_s * sign, (nseg, 1))
    tabs_p = _retention_tables(SUB, 1, CHUNK)
    tabs_s = _retention_tables(ls, nseg, min(CHUNK, ls))
    tiles_seq = lp // TILE
    n_ptiles = np_tok // TILE
    n_stiles = ns_tok // TILE
    p_tile = lambda i: jnp.minimum(i, n_ptiles - 1)
    s_tile = lambda i: jnp.maximum(i - n_ptiles, 0)
    tok_spec = lambda w: pl.BlockSpec((TILE, w), lambda i: (i, 0))
    tab_specs = [_const_spec((RET_HEADS, SUB, SUB)), _const_spec((RET_HEADS, SUB, RET_DK)),
                 _const_spec((RET_HEADS, SUB, RET_DK)), _const_spec((RET_HEADS, 8, RET_DK))]
    tile_seqs = SUBTILES * nseg
    state_s_spec = pl.BlockSpec((1, tile_seqs, RET_HEADS, RET_DK, RET_DK),
                                lambda i: (0, s_tile(i), 0, 0, 0))
    cache_s_spec = pl.BlockSpec((1, tile_seqs, CONV_K - 1, CONV_WIDTH), lambda i: (0, s_tile(i), 0, 0))
    h_all, hn_all, idx_t, wts_t, cnt_all, state_p, cache_p, state_s, cache_s = pl.pallas_call(
        functools.partial(_mixer_kernel, n_ptiles, tiles_seq, nseg, ls),
        grid=(n_ptiles + n_stiles,),
        in_specs=[pl.BlockSpec((TILE, D_MODEL), lambda i: (p_tile(i), 0)),
                  pl.BlockSpec((TILE, D_MODEL), lambda i: (s_tile(i), 0)),
                  _const_spec((lp // SUB, RET_DK)), _const_spec((lp // SUB, RET_DK))]
                 + [_const_spec((SUB, RET_DK))] * 4
                 + [_const_spec((SUB, RET_DK)), _const_spec((SUB, RET_DK))]
                 + tab_specs + tab_specs + [state_s_spec, cache_s_spec] + _weight_specs(),
        out_specs=[tok_spec(D_MODEL), pl.BlockSpec((TILE * TOKEN_ROWS, LANES), lambda i: (i, 0)),
                   pl.BlockSpec((TOP_K, TILE), lambda i: (0, i)),
                   pl.BlockSpec((TOP_K, TILE), lambda i: (0, i)),
                   pl.BlockSpec((N_EXPERTS, LANES), lambda i: (0, 0)),
                   pl.BlockSpec((1, 1, RET_HEADS, RET_DK, RET_DK),
                                lambda i: (0, p_tile(i) // tiles_seq, 0, 0, 0)),
                   pl.BlockSpec((1, 1, CONV_K - 1, CONV_WIDTH),
                                lambda i: (0, p_tile(i) // tiles_seq, 0, 0)),
                   state_s_spec, cache_s_spec],
        out_shape=[jax.ShapeDtypeStruct((n_tok, D_MODEL), F32),
                   jax.ShapeDtypeStruct((n_tok * TOKEN_ROWS, LANES), F32),
                   jax.ShapeDtypeStruct((TOP_K, n_tok), jnp.int32),
                   jax.ShapeDtypeStruct((TOP_K, n_tok), F32),
                   jax.ShapeDtypeStruct((N_EXPERTS, LANES), F32),
                   jax.ShapeDtypeStruct((1, bp, RET_HEADS, RET_DK, RET_DK), F32),
                   jax.ShapeDtypeStruct((1, bp, CONV_K - 1, CONV_WIDTH), F32),
                   jax.ShapeDtypeStruct(state_ret.shape, F32),
                   jax.ShapeDtypeStruct(cache_conv.shape, F32)],
        scratch_shapes=[pltpu.VMEM((SUBTILES, CONV_PAD + SUB, CONV_WIDTH), F32),
                        pltpu.VMEM((RET_HEADS, RET_DK, RET_DK), F32)],
        compiler_params=pltpu.CompilerParams(dimension_semantics=("arbitrary",),
                                             vmem_limit_bytes=MIXER_VMEM_LIMIT),
        name="mixer",
    )(x_prompt.reshape(np_tok, D_MODEL), x_sample.reshape(ns_tok, D_MODEL), *rope_p, cos_s, sin_s,
      *tabs_p, *tabs_s, state_ret, cache_conv, *weights)

    n_pair = n_tok * TOP_K
    n_tiles = n_pair // MOE_TILE + N_EXPERTS
    n_slot = n_tiles * MOE_TILE
    e_flat = idx_t.reshape(n_pair)
    counts = cnt_all[:, 0].astype(jnp.int32)
    tiles_e = (counts + MOE_TILE - 1) // MOE_TILE
    tile_end = jnp.cumsum(tiles_e)
    n_valid = tile_end[-1]
    tile_start = tile_end - tiles_e
    pair_bits = (n_pair - 1).bit_length()
    pad_mark = (1 << pair_bits) - 1
    assert pad_mark >= n_pair and (N_EXPERTS + 1) << pair_bits < 2 ** 31
    pad_end = jnp.cumsum(tiles_e * MOE_TILE - counts)
    pad_id = jnp.arange(n_slot - n_pair, dtype=jnp.int32)
    pad_e = jnp.sum((pad_end[None, :] <= pad_id[:, None]).astype(jnp.int32), axis=1)
    pair = jnp.arange(n_pair, dtype=jnp.int32)
    keys = jnp.concatenate([(e_flat << pair_bits) | pair, (pad_e << pair_bits) | pad_mark])
    slot_pair = lax.sort(keys, is_stable=False) & pad_mark
    slot_pair = jnp.where(slot_pair == pad_mark, 0, slot_pair)
    slot_src = (slot_pair % n_tok) * TOKEN_ROWS
    slot_dst = slot_pair * TOKEN_ROWS
    tile_id = jnp.arange(n_tiles, dtype=jnp.int32)
    tile_q = jnp.minimum(tile_id, n_valid - 1)
    tile_e = jnp.sum((tile_end[None, :] <= tile_q[:, None]).astype(jnp.int32), axis=1)
    tile_e = jnp.minimum(tile_e, N_EXPERTS - 1).astype(jnp.int32)
    of_tile = (tile_e[:, None] == jnp.arange(N_EXPERTS, dtype=jnp.int32)[None, :]).astype(jnp.int32)
    rows_left = jnp.sum(of_tile * (counts + tile_start * MOE_TILE)[None, :], axis=1) - tile_id * MOE_TILE
    tile_cnt = jnp.where(tile_id < n_valid, jnp.clip(rows_left, 0, MOE_TILE), 0).astype(jnp.int32)
    slot_src = slot_src.reshape(n_tiles, 1, MOE_TILE)
    slot_dst = slot_dst.reshape(n_tiles, 1, MOE_TILE)
    present = (counts > 0).astype(jnp.int32)
    grp_of_e = jnp.cumsum(present) - 1
    n_grp = grp_of_e[-1] + 1
    grp_id = jnp.arange(N_EXPERTS + 1, dtype=jnp.int32)
    expert_id = jnp.arange(N_EXPERTS, dtype=jnp.int32)
    grp_expert = jnp.sum(jnp.where((grp_of_e[None, :] == grp_id[:, None]) & (present[None, :] > 0),
                                   expert_id[None, :], 0), axis=1).astype(jnp.int32)
    tile_grp = jnp.sum(of_tile * grp_of_e[None, :], axis=1).astype(jnp.int32)

    any_spec = pl.BlockSpec(memory_space=pl.ANY)
    smem_spec = lambda fn: pl.BlockSpec((1, 1, MOE_TILE), fn, memory_space=pltpu.SMEM)
    b_spec = pl.BlockSpec((1, 1, D_MODEL), lambda j, te, *_: (te[j], 0, 0))
    y_rows = pl.pallas_call(
        _moe_kernel,
        grid_spec=pltpu.PrefetchScalarGridSpec(
            num_scalar_prefetch=6,
            grid=(n_tiles,),
            in_specs=[pl.BlockSpec((GATHER_AHEAD, 1, MOE_TILE), lambda j, *_: (0, 0, 0),
                                   memory_space=pltpu.SMEM),
                      smem_spec(lambda j, *_: (jnp.minimum(j + GATHER_AHEAD, n_tiles - 1), 0, 0)),
                      smem_spec(lambda j, *_: (jnp.maximum(j - 1, 0), 0, 0)),
                      any_spec, any_spec, any_spec, any_spec, b_spec, b_spec, b_spec],
            out_specs=any_spec,
            scratch_shapes=[pltpu.VMEM((X_BUFFERS, MOE_TILE * TOKEN_ROWS, LANES), F32),
                            pltpu.VMEM((2, MOE_TILE * TOKEN_ROWS, LANES), F32),
                            pltpu.VMEM((2, 3, D_MODEL, D_MODEL), F32),
                            pltpu.VMEM((D_MODEL, D_MODEL), BF16),
                            pltpu.VMEM((D_MODEL, D_MODEL), BF16),
                            pltpu.VMEM((D_MODEL, D_MODEL), BF16),
                            pltpu.SemaphoreType.DMA((X_BUFFERS,)),
                            pltpu.SemaphoreType.DMA((2,)),
                            pltpu.SemaphoreType.DMA((2,))]),
        out_shape=jax.ShapeDtypeStruct((n_pair * TOKEN_ROWS, LANES), F32),
        compiler_params=pltpu.CompilerParams(dimension_semantics=("arbitrary",),
                                             vmem_limit_bytes=VMEM_LIMIT),
        name="moe_experts",
    )(tile_e, tile_cnt, n_valid.reshape(1).astype(jnp.int32), tile_grp, grp_expert,
      n_grp.reshape(1).astype(jnp.int32), slot_src, slot_src, slot_dst, hn_all,
      w_e_gate[0], w_e_up[0], w_e_down[0],
      b_e_gate[0].reshape(N_EXPERTS, 1, D_MODEL), b_e_up[0].reshape(N_EXPERTS, 1, D_MODEL),
      b_e_down[0].reshape(N_EXPERTS, 1, D_MODEL))

    def combine(row0, rows):
        ct = COMBINE_TILE
        blk0 = row0 // ct
        planes = [pl.BlockSpec((ct * TOKEN_ROWS, LANES), functools.partial(
            lambda i, kk: (kk * (n_tok // ct) + blk0 + i, 0), kk=kk)) for kk in range(TOP_K)]
        return pl.pallas_call(
            _combine_kernel,
            grid=(rows // ct,),
            in_specs=[pl.BlockSpec((ct, D_MODEL), lambda i: (blk0 + i, 0))] + planes
                     + [pl.BlockSpec((ct, TOP_K), lambda i: (blk0 + i, 0)), _const_spec((1, D_MODEL))],
            out_specs=pl.BlockSpec((ct, D_MODEL), lambda i: (i, 0)),
            out_shape=jax.ShapeDtypeStruct((rows, D_MODEL), F32),
            compiler_params=pltpu.CompilerParams(dimension_semantics=("arbitrary",),
                                                 vmem_limit_bytes=VMEM_LIMIT),
            name="combine",
        )(h_all, y_rows, y_rows, y_rows, y_rows, wts_all, row(norm_final_g))

    wts_all = wts_t.T

    y_prompt = combine(0, np_tok).reshape(bp, lp, D_MODEL)
    y_sample = combine(np_tok, ns_tok).reshape(bs, ls, D_MODEL)
    return (y_prompt, y_sample, state_p, cache_p, state_s, cache_s)
```

```python
import functools

import jax
import jax.numpy as jnp
from jax import lax
from jax.experimental import pallas as pl
from jax.experimental.pallas import tpu as pltpu

D_MODEL = 1024
CHUNK = 64
RET_WIDTH = 512
RET_HEADS = 4
RET_DK = 128
CONV_WIDTH = 512
CONV_K = 31
IN_COLS = 4 * RET_WIDTH + 2 * CONV_WIDTH
ROPE_BASE = 10000.0
N_EXPERTS = 32
TOP_K = 4
SWIGLU_LIMIT = 7.0
SWIGLU_ALPHA = 1.702
EPS = 1e-6
PAST_LEN = 2048

LANES = 128
SUBLANES = 8
SUB = 256
SUBTILES = 2
TILE = SUB * SUBTILES
COMBINE_TILE = 512
CONV_PAD = 32
CONV_ROWS = 32
MOE_TILE = 256
DMA_UNROLL = 8
GATHER_AHEAD = 2
X_BUFFERS = GATHER_AHEAD + 1
VMEM_LIMIT = 52 * 1024 * 1024
MIXER_VMEM_LIMIT = 62 * 1024 * 1024

F32 = jnp.float32
BF16 = jnp.bfloat16


def _dot(a, b):
    return jnp.dot(a, b, preferred_element_type=F32)


def _dot_nt(a, b):
    return lax.dot_general(a, b, (((1,), (1,)), ((), ())), preferred_element_type=F32)


def _dot_tn(a, b):
    return lax.dot_general(a, b, (((0,), (0,)), ((), ())), preferred_element_type=F32)


def _rms(x, g):
    return x * lax.rsqrt(jnp.mean(x * x, axis=-1, keepdims=True) + EPS) * g


def _sigmoid(x):
    return 1.0 / (1.0 + jnp.exp(-x))


TOKEN_ROWS = D_MODEL // LANES
assert TOKEN_ROWS == SUBLANES


def _store_token_major(ref, val):
    t = val.shape[0]
    for c in range(TOKEN_ROWS):
        ref[pl.ds(c, t, stride=TOKEN_ROWS), :] = val[:, c * LANES:(c + 1) * LANES]


def _load_token_major(ref, t):
    return jnp.concatenate([ref[pl.ds(c, t, stride=TOKEN_ROWS), :] for c in range(TOKEN_ROWS)], axis=-1)


def _project(x_ref, gmix_ref, win_ref):
    xn = _rms(x_ref[...], gmix_ref[...]).astype(BF16)
    return _dot(xn, win_ref[:, 4 * RET_WIDTH:]), _dot(xn, win_ref[:, :4 * RET_WIDTH])


def _mixer_body(nseg, seg, first, x_ref, cosf, sinf, mask_ref, qdec_ref, kdec_ref, cdec_ref,
                s_in, c_in, gmix_ref, win_ref, convw_ref, convb_ref, lng_ref, lnb_ref, retg_ref,
                wout_ref, gffn_ref, wr_ref, br_ref,
                h_ref, hn_ref, idx_ref, wts_ref, cnt_ref, s_out, c_out, ubuf, s_scr,
                seg0=0, ubuf_prev=None, proj=None):
    carry = first is not None
    T = nseg * seg
    x = x_ref[...]
    proj_conv, proj = proj

    def fresh(val):
        return val if first is False else jnp.where(first, 0.0, val)

    if carry:
        ubuf[0:CONV_PAD, :] = fresh(ubuf_prev[seg:seg + CONV_PAD, :])

    o_heads = []
    for h in range(RET_HEADS):
        q = proj[:, h * RET_DK:(h + 1) * RET_DK]
        k = proj[:, RET_WIDTH + h * RET_DK:RET_WIDTH + (h + 1) * RET_DK]
        v = proj[:, 2 * RET_WIDTH + h * RET_DK:2 * RET_WIDTH + (h + 1) * RET_DK].astype(BF16)
        q = q * cosf + pltpu.roll(q, RET_DK // 2, 1) * sinf
        k = (k * cosf + pltpu.roll(k, RET_DK // 2, 1) * sinf) * (RET_DK ** -0.5)
        p = (_dot_nt(q.astype(BF16), k.astype(BF16)) * mask_ref[h]).astype(BF16)
        o = _dot(p, v)
        qd = (q * qdec_ref[h]).astype(BF16)
        kd = (k * kdec_ref[h]).astype(BF16)
        cdec = cdec_ref[h, 0:1, :]
        o_cross = []
        for s in range(nseg):
            rows = slice(s * seg, (s + 1) * seg)
            s_prev = fresh(s_scr[h]) if carry else s_in[0, seg0 + s, h]
            o_cross.append(_dot(qd[rows], s_prev.astype(BF16)))
            s_new = cdec * s_prev + _dot_tn(kd[rows], v[rows])
            if carry:
                s_scr[h] = s_new
                s_out[0, 0, h] = s_new
            else:
                s_out[0, seg0 + s, h] = s_new
        o = o + (o_cross[0] if nseg == 1 else jnp.concatenate(o_cross, axis=0))
        o = o * lax.rsqrt(jnp.mean(o * o, axis=-1, keepdims=True) + EPS)
        g = proj[:, 3 * RET_WIDTH + h * RET_DK:3 * RET_WIDTH + (h + 1) * RET_DK]
        o_heads.append(o * retg_ref[:, h * RET_DK:(h + 1) * RET_DK] * (g * _sigmoid(g)))

    a = proj_conv[:, :CONV_WIDTH]
    ga = proj_conv[:, CONV_WIDTH:]
    u = a * _sigmoid(ga)
    hist = CONV_PAD - (CONV_K - 1)
    cv_chunks = []
    for s in range(nseg):
        if not carry:
            ubuf[hist:CONV_PAD, :] = c_in[0, seg0 + s]
        ubuf[CONV_PAD:CONV_PAD + seg, :] = u[s * seg:(s + 1) * seg]
        for c in range(seg // CONV_ROWS):
            acc = jnp.broadcast_to(convb_ref[...], (CONV_ROWS, CONV_WIDTH))
            for shift in range(SUBLANES):
                rows = CONV_ROWS + (SUBLANES if shift else 0)
                part = None
                for j in range(CONV_K):
                    if (hist + j) % SUBLANES == shift:
                        r0 = c * CONV_ROWS + (hist + j) - shift
                        term = convw_ref[j:j + 1, :] * ubuf[r0:r0 + rows, :]
                        part = term if part is None else part + term
                acc = acc + part[shift:shift + CONV_ROWS, :]
            cv_chunks.append(acc)
        tail = ubuf[hist + seg:CONV_PAD + seg, :]
        if carry:
            c_out[0, 0] = tail
        else:
            c_out[0, seg0 + s] = tail
    cv = jnp.concatenate(cv_chunks, axis=0)
    mu = jnp.mean(cv, axis=-1, keepdims=True)
    xc = cv - mu
    ln = xc * lax.rsqrt(jnp.mean(xc * xc, axis=-1, keepdims=True) + EPS) * lng_ref[...] + lnb_ref[...]
    o_conv = ln * _sigmoid(ln)

    mixed = jnp.concatenate(o_heads + [o_conv], axis=-1).astype(BF16)
    hres = x + _dot(mixed, wout_ref[...])
    h_ref[...] = hres

    hn = _rms(hres, gffn_ref[...])
    _store_token_major(hn_ref, hn)
    hn_hi = hn.astype(BF16)
    hn_lo = (hn - hn_hi.astype(F32)).astype(BF16)
    r_hi = _dot_nt(wr_ref[...], hn_hi)
    r_lo = _dot_nt(wr_ref[...], hn_lo)
    logits = (r_hi[:N_EXPERTS] + r_hi[N_EXPERTS:2 * N_EXPERTS] + r_lo[:N_EXPERTS] + br_ref[...])
    expert = lax.broadcasted_iota(jnp.int32, (N_EXPERTS, T), 0).astype(F32)
    work = logits
    top_v, top_i = [], []
    picked = jnp.zeros((N_EXPERTS, T), F32)
    for kk in range(TOP_K):
        m = jnp.max(work, axis=0, keepdims=True)
        sel = jnp.min(jnp.where(work == m, expert, float(N_EXPERTS)), axis=0, keepdims=True)
        top_v.append(m)
        top_i.append(sel)
        hit = expert == sel
        picked = picked + hit.astype(F32)
        work = jnp.where(hit, -jnp.inf, work)
    ex = jnp.exp(jnp.concatenate(top_v, axis=0) - top_v[0])
    wts_ref[...] = ex / jnp.sum(ex, axis=0, keepdims=True)
    idx_ref[...] = jnp.concatenate(top_i, axis=0).astype(jnp.int32)
    cnt_ref[...] += jnp.broadcast_to(jnp.sum(picked, axis=1, keepdims=True), (N_EXPERTS, LANES))


def _mixer_kernel(n_prompt_tiles, tiles_per_seq, nseg, seg,
                  xp_ref, xs_ref, cb_ref, sb_ref, cr_ref, sr_ref, crs_ref, srs_ref, coss_ref, sins_ref,
                  maskp_ref, qdecp_ref, kdecp_ref, cdecp_ref,
                  masks_ref, qdecs_ref, kdecs_ref, cdecs_ref, s_in, c_in,
                  gmix_ref, win_ref, convw_ref, convb_ref, lng_ref, lnb_ref, retg_ref,
                  wout_ref, gffn_ref, wr_ref, br_ref,
                  h_ref, hn_ref, idx_ref, wts_ref, cnt_ref, sp_out, cp_out, ss_out, cs_out,
                  ubufs, s_scr):
    i = pl.program_id(0)
    weights = (gmix_ref, win_ref, convw_ref, convb_ref, lng_ref, lnb_ref, retg_ref,
               wout_ref, gffn_ref, wr_ref, br_ref)

    def rows(ref, k, per_row=1):
        return ref.at[pl.ds(k * SUB * per_row, SUB * per_row)]

    def outs(k):
        cols = pl.ds(k * SUB, SUB)
        return (rows(h_ref, k), rows(hn_ref, k, TOKEN_ROWS), idx_ref.at[:, cols], wts_ref.at[:, cols],
                cnt_ref)

    @pl.when(i == 0)
    def _():
        cnt_ref[...] = jnp.zeros_like(cnt_ref)
        s_scr[...] = jnp.zeros_like(s_scr)
        ubufs[...] = jnp.zeros_like(ubufs)

    @pl.when(i < n_prompt_tiles)
    def _():
        projs = [_project(rows(xp_ref, k), gmix_ref, win_ref) for k in range(SUBTILES)]
        for k in range(SUBTILES):
            first = (i % tiles_per_seq == 0) if k == 0 else False
            base = pl.ds((i % tiles_per_seq) * SUBTILES + k, 1)
            cos_b, sin_b = cb_ref[base, :], sb_ref[base, :]
            cosf = cos_b * cr_ref[...] - sin_b * sr_ref[...]
            sinf = sin_b * crs_ref[...] + cos_b * srs_ref[...]
            _mixer_body(1, SUB, first, rows(xp_ref, k), cosf, sinf,
                        maskp_ref, qdecp_ref, kdecp_ref, cdecp_ref, None, None, *weights, *outs(k),
                        sp_out, cp_out, ubufs.at[k], s_scr, ubuf_prev=ubufs.at[(k - 1) % SUBTILES],
                        proj=projs[k])

    @pl.when(i >= n_prompt_tiles)
    def _():
        projs = [_project(rows(xs_ref, k), gmix_ref, win_ref) for k in range(SUBTILES)]
        for k in range(SUBTILES):
            _mixer_body(nseg, seg, None, rows(xs_ref, k), coss_ref[...], sins_ref[...],
                        masks_ref, qdecs_ref, kdecs_ref, cdecs_ref, s_in, c_in, *weights, *outs(k),
                        ss_out, cs_out, ubufs.at[k], None, seg0=k * nseg, proj=projs[k])


def _retention_tables(seg, nseg, lc):
    T = seg * nseg
    lg = jnp.log(1.0 - 2.0 ** (-5.0 - jnp.arange(RET_HEADS, dtype=F32)))
    i = jnp.arange(T, dtype=jnp.int32)
    loc = (i % seg).astype(F32)
    same_seg = (i[:, None] // seg) == (i[None, :] // seg)
    causal = ((i[None, :] % seg) // lc) <= ((i[:, None] % seg) // lc)
    dist = jnp.abs(loc[:, None] - loc[None, :])
    mask = jnp.where((same_seg & causal)[None], jnp.exp(lg[:, None, None] * dist[None]), 0.0)
    qdec = jnp.exp(lg[:, None] * (loc[None, :] + 1.0))
    kdec = jnp.exp(lg[:, None] * (seg - 1.0 - loc[None, :]))
    cdec = jnp.exp(lg * seg)
    bc = lambda t: jnp.broadcast_to(t[:, :, None], (RET_HEADS, t.shape[1], RET_DK))
    return (mask.astype(F32), bc(qdec), bc(kdec),
            jnp.broadcast_to(cdec[:, None, None], (RET_HEADS, 8, RET_DK)))


def _rope_cos_sin(pos):
    half = RET_DK // 2
    inv = ROPE_BASE ** (-jnp.arange(half, dtype=F32) / half)
    ang = pos[:, None] * inv[None, :]
    cos, sin = jnp.cos(ang), jnp.sin(ang)
    return jnp.concatenate([cos, cos], axis=-1), jnp.concatenate([sin, sin], axis=-1)


def _rope_sign():
    half = RET_DK // 2
    return jnp.concatenate([-jnp.ones((half,), F32), jnp.ones((half,), F32)])[None, :]


def _const_spec(shape):
    return pl.BlockSpec(shape, lambda *_: (0,) * len(shape), pipeline_mode=pl.Buffered(1))


def _weight_specs():
    return [
        _const_spec((1, D_MODEL)),
        _const_spec((D_MODEL, IN_COLS)),
        _const_spec((CONV_K, CONV_WIDTH)),
        _const_spec((1, CONV_WIDTH)),
        _const_spec((1, CONV_WIDTH)),
        _const_spec((1, CONV_WIDTH)),
        _const_spec((1, RET_WIDTH)),
        _const_spec((D_MODEL, D_MODEL)),
        _const_spec((1, D_MODEL)),
        _const_spec((LANES, D_MODEL)),
        _const_spec((N_EXPERTS, SUB)),
    ]


def _moe_kernel(te_ref, cnt_ref, nv_ref, tgrp_ref, gexp_ref, ngrp_ref, first_ref, ahead_ref, dst_ref,
                hn_any, wg_any, wu_any, wd_any, bg_ref, bu_ref, bd_ref, y_any,
                xbuf, ybuf, w32, wg_bf, wu_bf, wd_bf, gsem, ssem, wsem):
    j = pl.program_id(0)
    nt = pl.num_programs(0)
    nvalid = nv_ref[0]
    slot = j % 2
    xslot = j % X_BUFFERS

    def token_rows(base, n=1):
        if not isinstance(base, int):
            base = pl.multiple_of(base, TOKEN_ROWS)
        return pl.ds(base, n * TOKEN_ROWS)

    def gather_copy(idx_ref, r, sl, t=0):
        return pltpu.make_async_copy(hn_any.at[token_rows(idx_ref[t, 0, r])],
                                     xbuf.at[sl, token_rows(r * TOKEN_ROWS)], gsem.at[sl])

    def scatter_copy(r, sl):
        return pltpu.make_async_copy(ybuf.at[sl, token_rows(r * TOKEN_ROWS)],
                                     y_any.at[token_rows(dst_ref[0, 0, r])], ssem.at[sl])

    def start_gather(idx_ref, sl, t=0):
        def body(r, c):
            gather_copy(idx_ref, r, sl, t).start()
            return c
        lax.fori_loop(0, MOE_TILE, body, 0, unroll=DMA_UNROLL)

    def wait_gather(sl):
        pltpu.make_async_copy(hn_any.at[token_rows(0, MOE_TILE)], xbuf.at[sl], gsem.at[sl]).wait()

    def start_scatter(tile):
        n = cnt_ref[tile]
        groups = n // DMA_UNROLL

        def group(g, c):
            for u in range(DMA_UNROLL):
                scatter_copy(g * DMA_UNROLL + u, tile % 2).start()
            return c
        lax.fori_loop(0, groups, group, 0)

        def single(r, c):
            scatter_copy(r, tile % 2).start()
            return c
        lax.fori_loop(groups * DMA_UNROLL, n, single, 0)

    def wait_scatter(tile):
        n = cnt_ref[tile]
        pltpu.make_async_copy(ybuf.at[tile % 2, token_rows(0, n)], y_any.at[token_rows(0, n)],
                              ssem.at[tile % 2]).wait()

    @pl.when(j == 0)
    def _():
        for t in range(GATHER_AHEAD):
            @pl.when(t < nvalid)
            def _():
                start_gather(first_ref, t, t)

    ahead_slot = (j + GATHER_AHEAD) % X_BUFFERS

    def compute(before_store, dma_starts=()):
        x = _load_token_major(xbuf.at[xslot], MOE_TILE).astype(BF16)
        for start in dma_starts:
            start()
        gate = jnp.minimum(_dot(x, wg_bf[...]) + bg_ref[0], SWIGLU_LIMIT)
        up = jnp.clip(_dot(x, wu_bf[...]) + bu_ref[0], -SWIGLU_LIMIT, SWIGLU_LIMIT)
        act = (gate * _sigmoid(SWIGLU_ALPHA * gate) * (up + 1.0)).astype(BF16)
        out = _dot(act, wd_bf[...]) + bd_ref[0]
        before_store()
        _store_token_major(ybuf.at[slot], out)

    @pl.when(j < nvalid)
    def _():
        group = tgrp_ref[j]
        half = group % 2

        def weight_copies(expert, dst_half):
            return [pltpu.make_async_copy(w_any.at[expert], w32.at[dst_half, m], wsem.at[dst_half])
                    for m, w_any in enumerate((wg_any, wu_any, wd_any))]

        @pl.when(jnp.logical_or(j == 0, tgrp_ref[jnp.maximum(j - 1, 0)] != group))
        def _():
            @pl.when(j == 0)
            def _():
                for copy in weight_copies(gexp_ref[0], 0):
                    copy.start()

            for copy in weight_copies(te_ref[j], half):
                copy.wait()
            for m, w_bf in enumerate((wg_bf, wu_bf, wd_bf)):
                w_bf[...] = w32[half, m].astype(BF16)

            @pl.when(group + 1 < ngrp_ref[0])
            def _():
                for copy in weight_copies(gexp_ref[group + 1], 1 - half):
                    copy.start()

        wait_gather(xslot)
        steady = jnp.logical_and(jnp.logical_and(j >= 2, j + GATHER_AHEAD < nvalid),
                                 cnt_ref[jnp.maximum(j - 1, 0)] == MOE_TILE)

        @pl.when(steady)
        def _():
            starts = []
            for r in range(MOE_TILE):
                starts.append(gather_copy(ahead_ref, r, ahead_slot).start)
                starts.append(functools.partial(scatter_copy(r, 1 - slot).start, priority=1))
            compute(lambda: wait_scatter(j - 2), starts)

        @pl.when(jnp.logical_not(steady))
        def _():
            @pl.when(j + GATHER_AHEAD < nvalid)
            def _():
                start_gather(ahead_ref, ahead_slot)

            @pl.when(j >= 1)
            def _():
                start_scatter(j - 1)

            def retire():
                @pl.when(j >= 2)
                def _():
                    wait_scatter(j - 2)
            compute(retire)

    @pl.when(j == nvalid)
    def _():
        start_scatter(j - 1)

    @pl.when(j == nt - 1)
    def _():
        wait_scatter(nvalid - 1)

        @pl.when(nvalid >= 2)
        def _():
            wait_scatter(nvalid - 2)


def _combine_kernel(h_ref, y0_ref, y1_ref, y2_ref, y3_ref, w_ref, g_ref, o_ref):
    w = w_ref[...]
    acc = h_ref[...]
    for kk, y_ref in enumerate((y0_ref, y1_ref, y2_ref, y3_ref)):
        acc = acc + w[:, kk:kk + 1] * _load_token_major(y_ref, COMBINE_TILE)
    o_ref[...] = _rms(acc, g_ref[...])


def kernel(x_prompt, x_sample, state_ret, cache_conv, norm_mix_g, w_in, conv_w, conv_b, conv_ln_g,
           conv_ln_b, ret_norm_g, w_out, norm_ffn_g, w_router, b_router, w_e_gate, b_e_gate, w_e_up,
           b_e_up, w_e_down, b_e_down, norm_final_g):
    bp, lp, _ = x_prompt.shape
    bs, ls, _ = x_sample.shape
    depth = norm_mix_g.shape[0]
    assert depth == 1 and lp % TILE == 0 and TILE % ls == 0 and (bs * ls) % TILE == 0
    np_tok, ns_tok = bp * lp, bs * ls
    n_tok = np_tok + ns_tok
    nseg = SUB // ls

    row = lambda t: t.reshape(1, -1)
    wr_hi = w_router[0].astype(BF16)
    wr_lo = (w_router[0] - wr_hi.astype(F32)).astype(BF16)
    wr_cat = jnp.concatenate(
        [wr_hi, wr_lo, jnp.zeros((D_MODEL, LANES - 2 * N_EXPERTS), BF16)], axis=1).T
    br_bcast = jnp.broadcast_to(b_router[0][:, None], (N_EXPERTS, SUB))
    weights = (row(norm_mix_g[0]), w_in[0].astype(BF16), conv_w[0], row(conv_b[0]), row(conv_ln_g[0]),
               row(conv_ln_b[0]), row(ret_norm_g[0]), w_out[0].astype(BF16), row(norm_ffn_g[0]),
               wr_cat, br_bcast)

    sign = _rope_sign()
    cos_b, sin_b = _rope_cos_sin(jnp.arange(lp // SUB, dtype=F32) * SUB)
    cos_r, sin_r = _rope_cos_sin(jnp.arange(SUB, dtype=F32))
    rope_p = (cos_b, sin_b, cos_r, sin_r, cos_r * sign, sin_r * sign)
    cos_s, sin_s = _rope_cos_sin(jnp.arange(ls, dtype=F32) + PAST_LEN)
    cos_s, sin_s = jnp.tile(cos_s, (nseg, 1)), jnp.tile(sin_s * sign, (nseg, 1))
    tabs_p = _retention_tables(SUB, 1, CHUNK)
    tabs_s = _retention_tables(ls, nseg, min(CHUNK, ls))
    tiles_seq = lp // TILE
    n_ptiles = np_tok // TILE
    n_stiles = ns_tok // TILE
    p_tile = lambda i: jnp.minimum(i, n_ptiles - 1)
    s_tile = lambda i: jnp.maximum(i - n_ptiles, 0)
    tok_spec = lambda w: pl.BlockSpec((TILE, w), lambda i: (i, 0))
    tab_specs = [_const_spec((RET_HEADS, SUB, SUB)), _const_spec((RET_HEADS, SUB, RET_DK)),
                 _const_spec((RET_HEADS, SUB, RET_DK)), _const_spec((RET_HEADS, 8, RET_DK))]
    tile_seqs = SUBTILES * nseg
    state_s_spec = pl.BlockSpec((1, tile_seqs, RET_HEADS, RET_DK, RET_DK),
                                lambda i: (0, s_tile(i), 0, 0, 0))
    cache_s_spec = pl.BlockSpec((1, tile_seqs, CONV_K - 1, CONV_WIDTH), lambda i: (0, s_tile(i), 0, 0))
    h_all, hn_all, idx_t, wts_t, cnt_all, state_p, cache_p, state_s, cache_s = pl.pallas_call(
        functools.partial(_mixer_kernel, n_ptiles, tiles_seq, nseg, ls),
        grid=(n_ptiles + n_stiles,),
        in_specs=[pl.BlockSpec((TILE, D_MODEL), lambda i: (p_tile(i), 0)),
                  pl.BlockSpec((TILE, D_MODEL), lambda i: (s_tile(i), 0)),
                  _const_spec((lp // SUB, RET_DK)), _const_spec((lp // SUB, RET_DK))]
                 + [_const_spec((SUB, RET_DK))] * 4
                 + [_const_spec((SUB, RET_DK)), _const_spec((SUB, RET_DK))]
                 + tab_specs + tab_specs + [state_s_spec, cache_s_spec] + _weight_specs(),
        out_specs=[tok_spec(D_MODEL), pl.BlockSpec((TILE * TOKEN_ROWS, LANES), lambda i: (i, 0)),
                   pl.BlockSpec((TOP_K, TILE), lambda i: (0, i)),
                   pl.BlockSpec((TOP_K, TILE), lambda i: (0, i)),
                   pl.BlockSpec((N_EXPERTS, LANES), lambda i: (0, 0)),
                   pl.BlockSpec((1, 1, RET_HEADS, RET_DK, RET_DK),
                                lambda i: (0, p_tile(i) // tiles_seq, 0, 0, 0)),
                   pl.BlockSpec((1, 1, CONV_K - 1, CONV_WIDTH),
                                lambda i: (0, p_tile(i) // tiles_seq, 0, 0)),
                   state_s_spec, cache_s_spec],
        out_shape=[jax.ShapeDtypeStruct((n_tok, D_MODEL), F32),
                   jax.ShapeDtypeStruct((n_tok * TOKEN_ROWS, LANES), F32),
                   jax.ShapeDtypeStruct((TOP_K, n_tok), jnp.int32),
                   jax.ShapeDtypeStruct((TOP_K, n_tok), F32),
                   jax.ShapeDtypeStruct((N_EXPERTS, LANES), F32),
                   jax.ShapeDtypeStruct((1, bp, RET_HEADS, RET_DK, RET_DK), F32),
                   jax.ShapeDtypeStruct((1, bp, CONV_K - 1, CONV_WIDTH), F32),
                   jax.ShapeDtypeStruct(state_ret.shape, F32),
                   jax.ShapeDtypeStruct(cache_conv.shape, F32)],
        scratch_shapes=[pltpu.VMEM((SUBTILES, CONV_PAD + SUB, CONV_WIDTH), F32),
                        pltpu.VMEM((RET_HEADS, RET_DK, RET_DK), F32)],
        compiler_params=pltpu.CompilerParams(dimension_semantics=("arbitrary",),
                                             vmem_limit_bytes=MIXER_VMEM_LIMIT),
        name="mixer",
    )(x_prompt.reshape(np_tok, D_MODEL), x_sample.reshape(ns_tok, D_MODEL), *rope_p, cos_s, sin_s,
      *tabs_p, *tabs_s, state_ret, cache_conv, *weights)

    n_pair = n_tok * TOP_K
    n_tiles = n_pair // MOE_TILE + N_EXPERTS
    n_slot = n_tiles * MOE_TILE
    e_flat = idx_t.reshape(n_pair)
    counts = cnt_all[:, 0].astype(jnp.int32)
    tiles_e = (counts + MOE_TILE - 1) // MOE_TILE
    tile_end = jnp.cumsum(tiles_e)
    n_valid = tile_end[-1]
    tile_start = tile_end - tiles_e
    pair_bits = (n_pair - 1).bit_length()
    pad_mark = (1 << pair_bits) - 1
    assert pad_mark >= n_pair and (N_EXPERTS + 1) << pair_bits < 2 ** 31
    pad_end = jnp.cumsum(tiles_e * MOE_TILE - counts)
    pad_id = jnp.arange(n_slot - n_pair, dtype=jnp.int32)
    pad_e = jnp.sum((pad_end[None, :] <= pad_id[:, None]).astype(jnp.int32), axis=1)
    pair = jnp.arange(n_pair, dtype=jnp.int32)
    keys = jnp.concatenate([(e_flat << pair_bits) | pair, (pad_e << pair_bits) | pad_mark])
    slot_pair = lax.sort(keys, is_stable=False) & pad_mark
    slot_pair = jnp.where(slot_pair == pad_mark, 0, slot_pair)
    slot_src = (slot_pair % n_tok) * TOKEN_ROWS
    slot_dst = slot_pair * TOKEN_ROWS
    tile_id = jnp.arange(n_tiles, dtype=jnp.int32)
    tile_q = jnp.minimum(tile_id, n_valid - 1)
    tile_e = jnp.sum((tile_end[None, :] <= tile_q[:, None]).astype(jnp.int32), axis=1)
    tile_e = jnp.minimum(tile_e, N_EXPERTS - 1).astype(jnp.int32)
    of_tile = (tile_e[:, None] == jnp.arange(N_EXPERTS, dtype=jnp.int32)[None, :]).astype(jnp.int32)
    rows_left = jnp.sum(of_tile * (counts + tile_start * MOE_TILE)[None, :], axis=1) - tile_id * MOE_TILE
    tile_cnt = jnp.where(tile_id < n_valid, jnp.clip(rows_left, 0, MOE_TILE), 0).astype(jnp.int32)
    slot_src = slot_src.reshape(n_tiles, 1, MOE_TILE)
    slot_dst = slot_dst.reshape(n_tiles, 1, MOE_TILE)
    present = (counts > 0).astype(jnp.int32)
    grp_of_e = jnp.cumsum(present) - 1
    n_grp = grp_of_e[-1] + 1
    grp_id = jnp.arange(N_EXPERTS + 1, dtype=jnp.int32)
    expert_id = jnp.arange(N_EXPERTS, dtype=jnp.int32)
    grp_expert = jnp.sum(jnp.where((grp_of_e[None, :] == grp_id[:, None]) & (present[None, :] > 0),
                                   expert_id[None, :], 0), axis=1).astype(jnp.int32)
    tile_grp = jnp.sum(of_tile * grp_of_e[None, :], axis=1).astype(jnp.int32)

    any_spec = pl.BlockSpec(memory_space=pl.ANY)
    smem_spec = lambda fn: pl.BlockSpec((1, 1, MOE_TILE), fn, memory_space=pltpu.SMEM)
    b_spec = pl.BlockSpec((1, 1, D_MODEL), lambda j, te, *_: (te[j], 0, 0))
    y_rows = pl.pallas_call(
        _moe_kernel,
        grid_spec=pltpu.PrefetchScalarGridSpec(
            num_scalar_prefetch=6,
            grid=(n_tiles,),
            in_specs=[pl.BlockSpec((GATHER_AHEAD, 1, MOE_TILE), lambda j, *_: (0, 0, 0),
                                   memory_space=pltpu.SMEM),
                      smem_spec(lambda j, *_: (jnp.minimum(j + GATHER_AHEAD, n_tiles - 1), 0, 0)),
                      smem_spec(lambda j, *_: (jnp.maximum(j - 1, 0), 0, 0)),
                      any_spec, any_spec, any_spec, any_spec, b_spec, b_spec, b_spec],
            out_specs=any_spec,
            scratch_shapes=[pltpu.VMEM((X_BUFFERS, MOE_TILE * TOKEN_ROWS, LANES), F32),
                            pltpu.VMEM((2, MOE_TILE * TOKEN_ROWS, LANES), F32),
                            pltpu.VMEM((2, 3, D_MODEL, D_MODEL), F32),
                            pltpu.VMEM((D_MODEL, D_MODEL), BF16),
                            pltpu.VMEM((D_MODEL, D_MODEL), BF16),
                            pltpu.VMEM((D_MODEL, D_MODEL), BF16),
                            pltpu.SemaphoreType.DMA((X_BUFFERS,)),
                            pltpu.SemaphoreType.DMA((2,)),
                            pltpu.SemaphoreType.DMA((2,))]),
        out_shape=jax.ShapeDtypeStruct((n_pair * TOKEN_ROWS, LANES), F32),
        compiler_params=pltpu.CompilerParams(dimension_semantics=("arbitrary",),
                                             vmem_limit_bytes=VMEM_LIMIT),
        name="moe_experts",
    )(tile_e, tile_cnt, n_valid.reshape(1).astype(jnp.int32), tile_grp, grp_expert,
      n_grp.reshape(1).astype(jnp.int32), slot_src, slot_src, slot_dst, hn_all,
      w_e_gate[0], w_e_up[0], w_e_down[0],
      b_e_gate[0].reshape(N_EXPERTS, 1, D_MODEL), b_e_up[0].reshape(N_EXPERTS, 1, D_MODEL),
      b_e_down[0].reshape(N_EXPERTS, 1, D_MODEL))

    def combine(row0, rows):
        ct = COMBINE_TILE
        blk0 = row0 // ct
        planes = [pl.BlockSpec((ct * TOKEN_ROWS, LANES), functools.partial(
            lambda i, kk: (kk * (n_tok // ct) + blk0 + i, 0), kk=kk)) for kk in range(TOP_K)]
        return pl.pallas_call(
            _combine_kernel,
            grid=(rows // ct,),
            in_specs=[pl.BlockSpec((ct, D_MODEL), lambda i: (blk0 + i, 0))] + planes
                     + [pl.BlockSpec((ct, TOP_K), lambda i: (blk0 + i, 0)), _const_spec((1, D_MODEL))],
            out_specs=pl.BlockSpec((ct, D_MODEL), lambda i: (i, 0)),
            out_shape=jax.ShapeDtypeStruct((rows, D_MODEL), F32),
            compiler_params=pltpu.CompilerParams(dimension_semantics=("arbitrary",),
                                                 vmem_limit_bytes=VMEM_LIMIT),
            name="combine",
        )(h_all, y_rows, y_rows, y_rows, y_rows, wts_all, row(norm_final_g))

    wts_all = wts_t.T

    y_prompt = combine(0, np_tok).reshape(bp, lp, D_MODEL)
    y_sample = combine(np_tok, ns_tok).reshape(bs, ls, D_MODEL)
    return (y_prompt, y_sample, state_p, cache_p, state_s, cache_s)
```
